```python
import math
import jax, jax.numpy as jnp
from jax import lax
import numpy as np

D_MODEL = 2048
BATCH = 2
SEQ = 4096
DEPTH = 2

SSM_GROUP = 16
SSM_STATE = 64
SSM_GROUPS = D_MODEL // SSM_GROUP
HEAD_DIM = 128
N_HEADS = D_MODEL // HEAD_DIM
D_FF = 11 * D_MODEL // 4
CONV_WIDTH = 3
BLOCK_Q = 128
N_A_LAYERS = max(1, DEPTH // 2)
N_B_LAYERS = DEPTH - N_A_LAYERS
N_MOD = 6
NORM_EPS = 1e-6
STEP_MIN = 1e-3
STEP_MAX = 1e-1
FORGET_BIAS_MEAN = 3.0

kernel_name = "yoco_s5_fox_convffn_adaln"


def rms_norm(x, g):
    x32 = x.astype(jnp.float32)
    y = x32 * lax.rsqrt(jnp.mean(x32 * x32, axis=-1, keepdims=True) + NORM_EPS)
    return (y * g.astype(jnp.float32)).astype(x.dtype)


def modulate(h, shift, scale):
    return h * (1 + scale[:, None, :]) + shift[:, None, :]


def _complex_affine_combine(e1, e2):
    a1r, a1i, b1r, b1i = e1
    a2r, a2i, b2r, b2i = e2
    ar = a2r * a1r - a2i * a1i
    ai = a2r * a1i + a2i * a1r
    br = a2r * b1r - a2i * b1i + b2r
    bi = a2r * b1i + a2i * b1r + b2i
    return (ar, ai, br, bi)


def s5_ssm(u, log_step, a_re, a_im, b_re, b_im, c_re, c_im, d):
    bsz, seq, _ = u.shape
    f32 = jnp.float32
    ug = u.astype(f32).reshape(bsz, seq, SSM_GROUPS, SSM_GROUP)
    lam_re = a_re.astype(f32)
    lam_im = a_im.astype(f32)
    step = jnp.exp(log_step.astype(f32))[:, None]
    mag = jnp.exp(lam_re * step)
    abar_re = mag * jnp.cos(lam_im * step)
    abar_im = mag * jnp.sin(lam_im * step)
    den = lam_re * lam_re + lam_im * lam_im
    nr = abar_re - 1.0
    fr = (nr * lam_re + abar_im * lam_im) / den
    fi = (abar_im * lam_re - nr * lam_im) / den
    br = b_re.astype(f32)
    bi = b_im.astype(f32)
    bbar_re = fr[..., None] * br - fi[..., None] * bi
    bbar_im = fr[..., None] * bi + fi[..., None] * br
    bu_re = jnp.einsum('blgh,gph->blgp', ug, bbar_re)
    bu_im = jnp.einsum('blgh,gph->blgp', ug, bbar_im)
    shape = bu_re.shape
    elems = (jnp.broadcast_to(abar_re, shape), jnp.broadcast_to(abar_im, shape), bu_re, bu_im)
    _, _, s_re, s_im = lax.associative_scan(_complex_affine_combine, elems, axis=1)
    y = (jnp.einsum('blgp,ghp->blgh', s_re, c_re.astype(f32))
         - jnp.einsum('blgp,ghp->blgh', s_im, c_im.astype(f32))
         + d.astype(f32) * ug)
    return y.reshape(bsz, seq, D_MODEL).astype(u.dtype)


def s5_mixer(h, w_in, log_step, a_re, a_im, b_re, b_im, c_re, c_im, d, glu_w, glu_b, w_out):
    u = h @ w_in
    y = s5_ssm(u, log_step, a_re, a_im, b_re, b_im, c_re, c_im, d)
    g = jax.nn.gelu(y)
    z = g * jax.nn.sigmoid(g @ glu_w + glu_b)
    return z @ w_out


def forgetting_attention(q, k, v, f_cum):
    seq = q.shape[1]
    scale = HEAD_DIM ** -0.5
    outs = []
    for start in range(0, seq, BLOCK_Q):
        end = start + BLOCK_Q
        qb = q[:, start:end]
        kb = k[:, :end]
        vb = v[:, :end]
        s = jnp.einsum('bqhd,bkhd->bhqk', qb, kb).astype(jnp.float32) * scale
        s = s + f_cum[:, :, start:end, None] - f_cum[:, :, None, :end]
        q_pos = start + jnp.arange(BLOCK_Q)
        k_pos = jnp.arange(end)
        mask = k_pos[None, :] <= q_pos[:, None]
        s = jnp.where(mask[None, None], s, -jnp.inf)
        p = jax.nn.softmax(s, axis=-1).astype(vb.dtype)
        outs.append(jnp.einsum('bhqk,bkhd->bqhd', p, vb))
    return jnp.concatenate(outs, axis=1)


def causal_depthwise_conv(a, w, b):
    ch = a.shape[-1]
    out = lax.conv_general_dilated(
        a, w[:, None, :].astype(a.dtype), window_strides=(1,),
        padding=[(CONV_WIDTH - 1, 0)],
        dimension_numbers=('NWC', 'WIO', 'NWC'),
        feature_group_count=ch)
    return out + b


def conv_ffn(h, w_up, conv_w, conv_b, w_down):
    a = h @ w_up
    a = causal_depthwise_conv(a, conv_w, conv_b)
    u, v = jnp.split(a, 2, axis=-1)
    return (jax.nn.silu(u) * v) @ w_down


def setup_inputs(seed: int = 0) -> dict:
    key = jax.random.key(seed)
    ks = jax.random.split(key, 32)
    f32 = jnp.float32
    D, G, P, H, F = D_MODEL, SSM_GROUPS, SSM_STATE, SSM_GROUP, D_FF
    nrm = lambda k, shape, s: jax.random.normal(k, shape, f32) * s
    inp = {}
    inp['x'] = nrm(ks[0], (BATCH, SEQ, D), 1.0)
    inp['c'] = nrm(ks[1], (BATCH, D), 1.0)
    inp['mod_w'] = nrm(ks[2], (DEPTH, D, N_MOD * D), 0.5 * D ** -0.5)
    inp['mod_b'] = nrm(ks[3], (DEPTH, N_MOD * D), 0.01)
    inp['norm_mix_g'] = 1.0 + nrm(ks[4], (DEPTH, D), 0.02)
    inp['norm_ffn_g'] = 1.0 + nrm(ks[5], (DEPTH, D), 0.02)
    inp['ssm_w_in'] = nrm(ks[6], (N_A_LAYERS, D, D), D ** -0.5)
    inp['ssm_log_step'] = jax.random.uniform(ks[7], (N_A_LAYERS, G), f32,
                                             math.log(STEP_MIN), math.log(STEP_MAX))
    inp['ssm_a_re'] = -0.5 + nrm(ks[8], (N_A_LAYERS, G, P), 0.01)
    inp['ssm_a_im'] = (math.pi * jnp.arange(P, dtype=f32))[None, None, :] + nrm(ks[9], (N_A_LAYERS, G, P), 0.01)
    inp['ssm_b_re'] = nrm(ks[10], (N_A_LAYERS, G, P, H), (2 * H) ** -0.5)
    inp['ssm_b_im'] = nrm(ks[11], (N_A_LAYERS, G, P, H), (2 * H) ** -0.5)
    inp['ssm_c_re'] = nrm(ks[12], (N_A_LAYERS, G, H, P), P ** -0.5)
    inp['ssm_c_im'] = nrm(ks[13], (N_A_LAYERS, G, H, P), P ** -0.5)
    inp['ssm_d'] = nrm(ks[14], (N_A_LAYERS, G, H), 1.0)
    inp['ssm_glu_w'] = nrm(ks[15], (N_A_LAYERS, D, D), D ** -0.5)
    inp['ssm_glu_b'] = nrm(ks[16], (N_A_LAYERS, D), 0.01)
    inp['ssm_w_out'] = nrm(ks[17], (N_A_LAYERS, D, D), D ** -0.5)
    inp['kv_mod_w'] = nrm(ks[18], (D, 2 * D), 0.5 * D ** -0.5)
    inp['kv_mod_b'] = nrm(ks[19], (2 * D,), 0.01)
    inp['kv_norm_g'] = 1.0 + nrm(ks[20], (D,), 0.02)
    inp['kv_w'] = nrm(ks[21], (D, 2 * D + N_HEADS), D ** -0.5)
    inp['forget_b'] = FORGET_BIAS_MEAN + nrm(ks[22], (N_HEADS,), 0.5)
    inp['attn_w_q'] = nrm(ks[23], (N_B_LAYERS, D, D), D ** -0.5)
    inp['attn_w_out'] = nrm(ks[24], (N_B_LAYERS, D, D), D ** -0.5)
    inp['ffn_w_up'] = nrm(ks[25], (DEPTH, D, 2 * F), D ** -0.5)
    inp['ffn_conv_w'] = nrm(ks[26], (DEPTH, CONV_WIDTH, 2 * F), CONV_WIDTH ** -0.5)
    inp['ffn_conv_b'] = nrm(ks[27], (DEPTH, 2 * F), 0.01)
    inp['ffn_w_down'] = nrm(ks[28], (DEPTH, F, D), F ** -0.5)
    inp['final_norm_g'] = 1.0 + nrm(ks[29], (D,), 0.02)
    return inp


def reference(x, c, mod_w, mod_b, norm_mix_g, norm_ffn_g,
              ssm_w_in, ssm_log_step, ssm_a_re, ssm_a_im, ssm_b_re, ssm_b_im,
              ssm_c_re, ssm_c_im, ssm_d, ssm_glu_w, ssm_glu_b, ssm_w_out,
              kv_mod_w, kv_mod_b, kv_norm_g, kv_w, forget_b,
              attn_w_q, attn_w_out,
              ffn_w_up, ffn_conv_w, ffn_conv_b, ffn_w_down,
              final_norm_g):
    bsz, seq, _ = x.shape
    c_act = jax.nn.silu(c)
    k_sh = v_sh = f_cum = None
    for i in range(DEPTH):
        mod = c_act @ mod_w[i] + mod_b[i]
        sh_m, sc_m, g_m, sh_f, sc_f, g_f = jnp.split(mod, N_MOD, axis=-1)
        if i == N_A_LAYERS:
            kv_mod = c_act @ kv_mod_w + kv_mod_b
            sh_kv, sc_kv = jnp.split(kv_mod, 2, axis=-1)
            hk = modulate(rms_norm(x, kv_norm_g), sh_kv, sc_kv)
            kvf = hk @ kv_w
            k_sh = kvf[..., :D_MODEL].reshape(bsz, seq, N_HEADS, HEAD_DIM)
            v_sh = kvf[..., D_MODEL:2 * D_MODEL].reshape(bsz, seq, N_HEADS, HEAD_DIM)
            log_f = jax.nn.log_sigmoid(kvf[..., 2 * D_MODEL:].astype(jnp.float32)
                                       + forget_b.astype(jnp.float32))
            f_cum = jnp.transpose(jnp.cumsum(log_f, axis=1), (0, 2, 1))
        h = modulate(rms_norm(x, norm_mix_g[i]), sh_m, sc_m)
        if i < N_A_LAYERS:
            out = s5_mixer(h, ssm_w_in[i], ssm_log_step[i], ssm_a_re[i], ssm_a_im[i],
                           ssm_b_re[i], ssm_b_im[i], ssm_c_re[i], ssm_c_im[i], ssm_d[i],
                           ssm_glu_w[i], ssm_glu_b[i], ssm_w_out[i])
        else:
            j = i - N_A_LAYERS
            q = (h @ attn_w_q[j]).reshape(bsz, seq, N_HEADS, HEAD_DIM)
            o = forgetting_attention(q, k_sh, v_sh, f_cum)
            out = o.reshape(bsz, seq, D_MODEL) @ attn_w_out[j]
        x = x + g_m[:, None, :] * out
        h = modulate(rms_norm(x, norm_ffn_g[i]), sh_f, sc_f)
        out = conv_ffn(h, ffn_w_up[i], ffn_conv_w[i], ffn_conv_b[i], ffn_w_down[i])
        x = x + g_f[:, None, :] * out
    return rms_norm(x, final_norm_g)
```

```python
import functools
import math

import jax
import jax.numpy as jnp
from jax import lax
from jax.experimental import pallas as pl
from jax.experimental.pallas import tpu as pltpu

F32 = jnp.float32
BF16 = jnp.bfloat16

LANES = 128
SUBLANES = 8
BF16_ROWS = 16
VMEM_LIMIT_BYTES = 56 * 1024 * 1024

SSM_GROUP = 16
SSM_STATE = 64
HEAD_DIM = 128
CONV_WIDTH = 3
NORM_EPS = 1e-6
N_MOD = 6

GROUPS_PER_BLOCK = LANES // SSM_GROUP
STATE_LANES = GROUPS_PER_BLOCK * SSM_STATE
SCAN_STEPS = (1, 2, 4)


def _params(*sem):
    return pltpu.CompilerParams(dimension_semantics=sem, vmem_limit_bytes=VMEM_LIMIT_BYTES)


def _tile(n, pref, align):
    if n <= pref:
        return n
    t = (pref // align) * align
    while t >= align:
        if n % t == 0:
            return t
        t -= align
    raise ValueError(f"no tile for {n}")


def _rms_hat(x):
    ms = jnp.mean(x * x, axis=-1, keepdims=True)
    return x * lax.rsqrt(ms + NORM_EPS)


def _modulate(y, shift, scale):
    return y * (1 + scale) + shift


def _gemv_kernel(c_ref, w_ref, b_ref, o_ref, cs_ref):
    @pl.when(pl.program_id(0) == 0)
    def _():
        cv = c_ref[...]
        cs_ref[...] = cv * jax.nn.sigmoid(cv)

    tn = w_ref.shape[1]
    for b in range(c_ref.shape[0]):
        cb = cs_ref[b]
        for lt in range(tn // LANES):
            sl = slice(lt * LANES, (lt + 1) * LANES)
            o_ref[b:b + 1, sl] = jnp.sum(w_ref[:, sl] * cb, axis=0, keepdims=True) + b_ref[:, sl]


def _gemv(c_rep, w, b):
    bsz, k, _ = c_rep.shape
    n = w.shape[1]
    tn = _tile(n, 1024, LANES)
    return pl.pallas_call(
        _gemv_kernel,
        out_shape=jax.ShapeDtypeStruct((bsz, n), F32),
        grid=(n // tn,),
        in_specs=[pl.BlockSpec((bsz, k, LANES), lambda j: (0, 0, 0)),
                  pl.BlockSpec((k, tn), lambda j: (0, j)),
                  pl.BlockSpec((1, tn), lambda j: (0, j))],
        out_specs=pl.BlockSpec((bsz, tn), lambda j: (0, j)),
        scratch_shapes=[pltpu.VMEM((bsz, k, LANES), F32)],
        compiler_params=_params("arbitrary"),
        name="adaln_gemv",
    )(c_rep, w, b.reshape(1, n))


def _norm_mod_kernel(x_ref, g_ref, sh_ref, sc_ref, o_ref):
    y = _rms_hat(x_ref[...]) * g_ref[...]
    o_ref[...] = _modulate(y, sh_ref[0], sc_ref[0]).astype(o_ref.dtype)


def _norm_mod(x2, g, shift, scale, seq):
    n, d = x2.shape
    tm = _tile(seq, 512, BF16_ROWS)
    tpb = seq // tm
    vec = pl.BlockSpec((1, 1, d), lambda i: (i // tpb, 0, 0))
    return pl.pallas_call(
        _norm_mod_kernel,
        out_shape=jax.ShapeDtypeStruct((n, d), BF16),
        grid=(n // tm,),
        in_specs=[pl.BlockSpec((tm, d), lambda i: (i, 0)),
                  pl.BlockSpec((1, d), lambda i: (0, 0)), vec, vec],
        out_specs=pl.BlockSpec((tm, d), lambda i: (i, 0)),
        compiler_params=_params("arbitrary"),
        name="norm_mod",
    )(x2, g.reshape(1, d), shift, scale)


def _mm_kernel(a_ref, w_ref, *rest, glu, out_scale):
    acc = jnp.dot(a_ref[...], w_ref[...], preferred_element_type=F32)
    if glu:
        g_ref, b_ref, o_ref = rest
        o_ref[...] = (g_ref[...].astype(F32) * jax.nn.sigmoid(acc + b_ref[...])).astype(o_ref.dtype)
    else:
        (o_ref,) = rest
        if out_scale != 1.0:
            acc = acc * out_scale
        o_ref[...] = acc.astype(o_ref.dtype)


def _mm(a, w, *, glu_bias=None, out_scale=1.0, name):
    n, k = a.shape
    nout = w.shape[1]
    tm = _tile(n, 1024, BF16_ROWS)
    tn = _tile(nout, 1024, LANES)
    in_specs = [pl.BlockSpec((tm, k), lambda j, i: (i, 0)),
                pl.BlockSpec((k, tn), lambda j, i: (0, j))]
    args = [a, w]
    if glu_bias is not None:
        in_specs += [pl.BlockSpec((tm, tn), lambda j, i: (i, j)),
                     pl.BlockSpec((1, tn), lambda j, i: (0, j))]
        args += [a, glu_bias.reshape(1, nout)]
    return pl.pallas_call(
        functools.partial(_mm_kernel, glu=glu_bias is not None, out_scale=out_scale),
        out_shape=jax.ShapeDtypeStruct((n, nout), BF16),
        grid=(nout // tn, n // tm),
        in_specs=in_specs,
        out_specs=pl.BlockSpec((tm, tn), lambda j, i: (i, j)),
        compiler_params=_params("arbitrary", "arbitrary"),
        name=name,
    )(*args)


def _norm_arg_count(kinds):
    return sum(3 if kd == "mod" else 1 for kd in kinds)


def _emit_norms(xn, kinds, norm_refs, out_refs):
    xhat = _rms_hat(xn)
    first = None
    p = 0
    for kd, o_ref in zip(kinds, out_refs):
        if kd == "mod":
            g_ref, sh_ref, sc_ref = norm_refs[p:p + 3]
            p += 3
            val = _modulate(xhat * g_ref[...], sh_ref[0], sc_ref[0])
        else:
            val = xhat * norm_refs[p][...]
            p += 1
        if first is None:
            first = val
        o_ref[...] = val.astype(o_ref.dtype)
    return first


def _norm_specs_args(kinds, norms, d, bidx):
    specs, args = [], []
    for kd, nm in zip(kinds, norms):
        specs.append(pl.BlockSpec((1, d), lambda *ix: (0, 0)))
        args.append(nm[0].reshape(1, d))
        if kd == "mod":
            for v in nm[1:]:
                specs.append(pl.BlockSpec((1, 1, d), bidx))
                args.append(v)
    return specs, args


def _norm_outs(kinds, n, d, tm, row_map):
    shapes = [jax.ShapeDtypeStruct((n, d), BF16 if kd == "mod" else F32) for kd in kinds]
    specs = [pl.BlockSpec((tm, d), row_map) for _ in kinds]
    return shapes, specs


def _mm_res_kernel(a_ref, w_ref, x_ref, gate_ref, *rest, kinds):
    na = _norm_arg_count(kinds)
    norm_refs, outs = rest[:na], rest[na:]
    acc = jnp.dot(a_ref[...], w_ref[...], preferred_element_type=F32)
    xn = x_ref[...] + gate_ref[0] * acc
    outs[0][...] = xn
    _emit_norms(xn, kinds, norm_refs, outs[1:])


def _mm_res(a, w, x2, gate, kinds, norms, seq, *, name):
    n, k = a.shape
    d = w.shape[1]
    tm = _tile(seq, 512, BF16_ROWS)
    tpb = seq // tm
    bidx = lambda i: (i // tpb, 0, 0)
    row = lambda i: (i, 0)
    nspecs, nargs = _norm_specs_args(kinds, norms, d, bidx)
    oshapes, ospecs = _norm_outs(kinds, n, d, tm, row)
    return pl.pallas_call(
        functools.partial(_mm_res_kernel, kinds=tuple(kinds)),
        out_shape=[jax.ShapeDtypeStruct((n, d), F32)] + oshapes,
        grid=(n // tm,),
        in_specs=[pl.BlockSpec((tm, k), row),
                  pl.BlockSpec((k, d), lambda i: (0, 0)),
                  pl.BlockSpec((tm, d), row),
                  pl.BlockSpec((1, 1, d), bidx)] + nspecs,
        out_specs=[pl.BlockSpec((tm, d), row)] + ospecs,
        compiler_params=_params("arbitrary"),
        name=name,
    )(a, w, x2, gate, *nargs)


def _ssm_kernel(u_ref, bp_ref, cp_ref, coef_ref, d_ref, o_ref, st_ref, carry_ref):
    s = STATE_LANES
    tt = u_ref.shape[0]

    @pl.when(pl.program_id(2) == 0)
    def _():
        carry_ref[...] = jnp.zeros_like(carry_ref)

    u = u_ref[...]
    st_ref[...] = jnp.dot(u, bp_ref[0], preferred_element_type=F32)
    steps = [(dlt, coef_ref[0, 2 * q], coef_ref[0, 2 * q + 1]) for q, dlt in enumerate(SCAN_STEPS)]
    pr = coef_ref[0, 2 * len(SCAN_STEPS)]
    pi = coef_ref[0, 2 * len(SCAN_STEPS) + 1]

    def body(i, carry):
        cr, ci = carry
        r0 = pl.multiple_of(i * SUBLANES, SUBLANES)
        xr = st_ref[pl.ds(r0, SUBLANES), 0:s]
        xi = st_ref[pl.ds(r0, SUBLANES), s:2 * s]
        for dlt, ar, ai in steps:
            sr = pltpu.roll(xr, dlt, 0)
            si = pltpu.roll(xi, dlt, 0)
            xr, xi = xr + (ar * sr - ai * si), xi + (ar * si + ai * sr)
        xr, xi = xr + (pr * cr - pi * ci), xi + (pr * ci + pi * cr)
        st_ref[pl.ds(r0, SUBLANES), 0:s] = xr
        st_ref[pl.ds(r0, SUBLANES), s:2 * s] = xi
        last = SUBLANES - 1
        return (jnp.broadcast_to(xr[last:last + 1], xr.shape),
                jnp.broadcast_to(xi[last:last + 1], xi.shape))

    cr, ci = lax.fori_loop(0, tt // SUBLANES, body, (carry_ref[0], carry_ref[1]), unroll=2)
    carry_ref[0] = cr
    carry_ref[1] = ci
    y = jnp.dot(st_ref[...].astype(BF16), cp_ref[0], preferred_element_type=F32)
    y = y + d_ref[...] * u.astype(F32)
    o_ref[...] = jax.nn.gelu(y).astype(o_ref.dtype)


def _s5_pack(log_step, a_re, a_im, b_re, b_im, c_re, c_im):
    g, p = a_re.shape
    h = b_re.shape[-1]
    nblk = g // GROUPS_PER_BLOCK
    step = jnp.exp(log_step.astype(F32))[:, None]
    lam_re, lam_im = a_re.astype(F32), a_im.astype(F32)

    def power(k):
        mag = jnp.exp(k * lam_re * step)
        return mag * jnp.cos(k * lam_im * step), mag * jnp.sin(k * lam_im * step)

    abar_re, abar_im = power(1)
    den = lam_re * lam_re + lam_im * lam_im
    nr = abar_re - 1.0
    fr = (nr * lam_re + abar_im * lam_im) / den
    fi = (abar_im * lam_re - nr * lam_im) / den
    br, bi = b_re.astype(F32), b_im.astype(F32)
    bbar_re = fr[..., None] * br - fi[..., None] * bi
    bbar_im = fr[..., None] * bi + fi[..., None] * br
    eye = jnp.eye(GROUPS_PER_BLOCK, dtype=F32)

    def pack_b(bb):
        t = bb.reshape(nblk, GROUPS_PER_BLOCK, p, h)
        t = jnp.einsum("jgph,gk->jghkp", t, eye)
        return t.reshape(nblk, LANES, STATE_LANES)

    def pack_c(cc):
        t = cc.astype(F32).reshape(nblk, GROUPS_PER_BLOCK, h, p)
        t = jnp.einsum("jghp,gk->jgpkh", t, eye)
        return t.reshape(nblk, STATE_LANES, LANES)

    bpack = jnp.concatenate([pack_b(bbar_re), pack_b(bbar_im)], axis=-1).astype(BF16)
    cpack = jnp.concatenate([pack_c(c_re), -pack_c(c_im)], axis=1).astype(BF16)

    rows = jnp.arange(SUBLANES)

    def lanes(v):
        return v.reshape(nblk, 1, STATE_LANES)

    coefs = []
    for dlt in SCAN_STEPS:
        keep = (rows >= dlt).astype(F32)[None, :, None]
        pre, pim = power(float(dlt))
        coefs += [lanes(pre) * keep, lanes(pim) * keep]
    pw = [power(float(r + 1)) for r in range(SUBLANES)]
    coefs.append(jnp.stack([lanes(q[0])[:, 0] for q in pw], axis=1))
    coefs.append(jnp.stack([lanes(q[1])[:, 0] for q in pw], axis=1))
    coef = jnp.stack(coefs, axis=1)
    return bpack, cpack, coef


def _ssm(u, bpack, cpack, coef, dvec, bsz, seq):
    n, d = u.shape
    nblk = d // LANES
    tt = _tile(seq, 1024, BF16_ROWS)
    nt = seq // tt
    row = lambda b, j, t: (b * nt + t, j)
    return pl.pallas_call(
        _ssm_kernel,
        out_shape=jax.ShapeDtypeStruct((n, d), BF16),
        grid=(bsz, nblk, nt),
        in_specs=[pl.BlockSpec((tt, LANES), row),
                  pl.BlockSpec((1, LANES, 2 * STATE_LANES), lambda b, j, t: (j, 0, 0)),
                  pl.BlockSpec((1, 2 * STATE_LANES, LANES), lambda b, j, t: (j, 0, 0)),
                  pl.BlockSpec((1, 2 * len(SCAN_STEPS) + 2, SUBLANES, STATE_LANES),
                               lambda b, j, t: (j, 0, 0, 0)),
                  pl.BlockSpec((1, LANES), lambda b, j, t: (0, j))],
        out_specs=pl.BlockSpec((tt, LANES), row),
        scratch_shapes=[pltpu.VMEM((tt, 2 * STATE_LANES), F32),
                        pltpu.VMEM((2, SUBLANES, STATE_LANES), F32)],
        compiler_params=_params("arbitrary", "arbitrary", "arbitrary"),
        name="s5_ssm",
    )(u, bpack, cpack, coef, dvec.reshape(1, d))


def _ffn_kernel(h_ref, halo_ref, wu_ref, wv_ref, cwu_ref, cwv_ref, cbu_ref, cbv_ref, wd_ref,
                x_ref, gate_ref, *rest, kinds, emit_x, with_flog, tiles_per_batch):
    na = _norm_arg_count(kinds)
    norm_refs = rest[:na]
    rest = rest[na:]
    if with_flog:
        wf_ref, rest = rest[0], rest[1:]
    n_out = len(kinds) + int(emit_x) + int(with_flog)
    outs, (hx_ref,) = rest[:n_out], rest[n_out:]
    acc_ref = outs[0]
    i, f = pl.program_id(0), pl.program_id(1)
    halo = halo_ref.shape[0]

    @pl.when(f == 0)
    def _():
        first = (i % tiles_per_batch) == 0
        hv = halo_ref[...]
        hx_ref[0:halo] = jnp.where(first, jnp.zeros_like(hv), hv)
        hx_ref[halo:] = h_ref[...]

    hx = hx_ref[...]

    def conv(w_ref, cw_ref, cb_ref):
        a = jnp.dot(hx, w_ref[...], preferred_element_type=F32)
        cw = cw_ref[...]
        out = cw[CONV_WIDTH - 1:CONV_WIDTH] * a[halo:]
        for lag in range(1, CONV_WIDTH):
            tap = CONV_WIDTH - 1 - lag
            out = out + cw[tap:tap + 1] * pltpu.roll(a, lag, 0)[halo:]
        return out + cb_ref[...]

    cu = conv(wu_ref, cwu_ref, cbu_ref)
    cv = conv(wv_ref, cwv_ref, cbv_ref)
    act = (cu * jax.nn.sigmoid(cu) * cv).astype(BF16)
    contrib = jnp.dot(act, wd_ref[...], preferred_element_type=F32)

    @pl.when(f == 0)
    def _():
        acc_ref[...] = contrib

    @pl.when(f > 0)
    def _():
        acc_ref[...] += contrib

    @pl.when(f == pl.num_programs(1) - 1)
    def _():
        xn = x_ref[...] + gate_ref[0] * acc_ref[...]
        k = 0
        if emit_x:
            outs[0][...] = xn
            k = 1
        first = _emit_norms(xn, kinds, norm_refs, outs[k:k + len(kinds)])
        if with_flog:
            outs[-1][...] = lax.dot_general(
                wf_ref[...], first, (((1,), (1,)), ((), ())),
                precision=lax.Precision.HIGHEST, preferred_element_type=F32)


def _ffn(h, x2, w_up, conv_w, conv_b, w_down, gate, kinds, norms, seq, *, emit_x, wf_t=None, name):
    n, d = h.shape
    ff = w_down.shape[0]
    tm = _tile(seq, 512, BF16_ROWS)
    tf = _tile(ff, 512, LANES)
    nf = ff // tf
    tpb = seq // tm
    halo = BF16_ROWS
    hpt = tm // halo
    row = lambda i, f: (i, 0)
    bidx = lambda i, f: (i // tpb, 0, 0)
    nspecs, nargs = _norm_specs_args(kinds, norms, d, bidx)
    oshapes, ospecs = _norm_outs(kinds, n, d, tm, row)
    if emit_x:
        oshapes = [jax.ShapeDtypeStruct((n, d), F32)] + oshapes
        ospecs = [pl.BlockSpec((tm, d), row)] + ospecs
    assert oshapes[0].dtype == F32
    extra_specs, extra_args = [], []
    if wf_t is not None:
        nh = wf_t.shape[0]
        extra_specs = [pl.BlockSpec((nh, d), lambda i, f: (0, 0))]
        extra_args = [wf_t]
        oshapes = oshapes + [jax.ShapeDtypeStruct((nh, n), F32)]
        ospecs = ospecs + [pl.BlockSpec((nh, tm), lambda i, f: (0, i))]
    cw = conv_w.astype(F32)
    cb = conv_b.astype(F32).reshape(1, 2 * ff)
    return pl.pallas_call(
        functools.partial(_ffn_kernel, kinds=tuple(kinds), emit_x=emit_x,
                          with_flog=wf_t is not None, tiles_per_batch=tpb),
        out_shape=oshapes,
        grid=(n // tm, nf),
        in_specs=[pl.BlockSpec((tm, d), row),
                  pl.BlockSpec((halo, d), lambda i, f: (jnp.maximum(i * hpt - 1, 0), 0)),
                  pl.BlockSpec((d, tf), lambda i, f: (0, f)),
                  pl.BlockSpec((d, tf), lambda i, f: (0, nf + f)),
                  pl.BlockSpec((CONV_WIDTH, tf), lambda i, f: (0, f)),
                  pl.BlockSpec((CONV_WIDTH, tf), lambda i, f: (0, nf + f)),
                  pl.BlockSpec((1, tf), lambda i, f: (0, f)),
                  pl.BlockSpec((1, tf), lambda i, f: (0, nf + f)),
                  pl.BlockSpec((tf, d), lambda i, f: (f, 0)),
                  pl.BlockSpec((tm, d), row),
                  pl.BlockSpec((1, 1, d), bidx)] + nspecs + extra_specs,
        out_specs=ospecs,
        scratch_shapes=[pltpu.VMEM((tm + halo, d), BF16)],
        compiler_params=_params("arbitrary", "arbitrary"),
        name=name,
    )(h, h, w_up, w_up, cw, cw, cb, cb, w_down, x2, gate, *nargs, *extra_args)


def _fcum_kernel(z_ref, b_ref, tri_ref, o_ref):
    z = z_ref[...] + b_ref[...]
    lf = jnp.minimum(z, 0.0) - jnp.log1p(jnp.exp(-jnp.abs(z)))
    carry = jnp.zeros((z.shape[0], 1), F32)
    for cidx in range(z.shape[1] // LANES):
        sl = slice(cidx * LANES, (cidx + 1) * LANES)
        cs = jnp.dot(lf[:, sl], tri_ref[...], precision=lax.Precision.HIGHEST,
                     preferred_element_type=F32) + carry
        o_ref[:, sl] = cs
        carry = cs[:, LANES - 1:LANES]


def _fcum(flog_t, forget_b, bsz, seq):
    nh, n = flog_t.shape
    tri = (jnp.arange(LANES)[:, None] <= jnp.arange(LANES)[None, :]).astype(F32)
    return pl.pallas_call(
        _fcum_kernel,
        out_shape=jax.ShapeDtypeStruct((nh, n), F32),
        grid=(bsz,),
        in_specs=[pl.BlockSpec((nh, seq), lambda b: (0, b)),
                  pl.BlockSpec((nh, 1), lambda b: (0, 0)),
                  pl.BlockSpec((LANES, LANES), lambda b: (0, 0))],
        out_specs=pl.BlockSpec((nh, seq), lambda b: (0, b)),
        compiler_params=_params("arbitrary"),
        name="forget_cumsum",
    )(flog_t, forget_b.astype(F32).reshape(nh, 1), tri)


def _attn_kernel(q_ref, k_ref, v_ref, f_ref, o_ref, m_ref, l_ref, acc_ref, *, tk):
    tq = q_ref.shape[0]
    qi = pl.program_id(2)
    q = q_ref[...]
    m_ref[...] = jnp.full_like(m_ref, -jnp.inf)
    l_ref[...] = jnp.zeros_like(l_ref)
    acc_ref[...] = jnp.zeros_like(acc_ref)

    def step(cidx, masked):
        k0 = pl.multiple_of(cidx * tk, tk)
        kc = k_ref[pl.ds(k0, tk), :]
        vc = v_ref[pl.ds(k0, tk), :]
        s = lax.dot_general(q, kc, (((1,), (1,)), ((), ())), preferred_element_type=F32)
        s = s - f_ref[0, :, pl.ds(k0, tk)]
        if masked:
            rows = qi * tq + lax.broadcasted_iota(jnp.int32, (tq, tk), 0)
            cols = k0 + lax.broadcasted_iota(jnp.int32, (tq, tk), 1)
            s = jnp.where(cols <= rows, s, -jnp.inf)
        m_old = m_ref[...]
        m_new = jnp.maximum(m_old, jnp.max(s, axis=-1, keepdims=True))
        alpha = jnp.exp(m_old - m_new)
        p = jnp.exp(s - m_new)
        l_ref[...] = alpha * l_ref[...] + jnp.sum(p, axis=-1, keepdims=True)
        acc_ref[...] = alpha * acc_ref[...] + jnp.dot(p.astype(BF16), vc, preferred_element_type=F32)
        m_ref[...] = m_new

    n_diag = tq // tk
    n_full = qi * n_diag

    def full_body(cidx, carry):
        step(cidx, False)
        return carry

    lax.fori_loop(0, n_full, full_body, 0)
    for dg in range(n_diag):
        step(n_full + dg, True)
    o_ref[...] = (acc_ref[...] / l_ref[...]).astype(o_ref.dtype)


def _attention(q, kv, f_cum, bsz, seq):
    n, d = q.shape
    nh = d // HEAD_DIM
    tq = _tile(seq, 512, LANES)
    tk = tq
    nq = seq // tq
    return pl.pallas_call(
        functools.partial(_attn_kernel, tk=tk),
        out_shape=jax.ShapeDtypeStruct((n, d), BF16),
        grid=(bsz, nh, nq),
        in_specs=[pl.BlockSpec((tq, HEAD_DIM), lambda b, h, i: (b * nq + i, h)),
                  pl.BlockSpec((seq, HEAD_DIM), lambda b, h, i: (b, h)),
                  pl.BlockSpec((seq, HEAD_DIM), lambda b, h, i: (b, nh + h)),
                  pl.BlockSpec((1, 1, seq), lambda b, h, i: (h, 0, b))],
        out_specs=pl.BlockSpec((tq, HEAD_DIM), lambda b, h, i: (b * nq + i, h)),
        scratch_shapes=[pltpu.VMEM((tq, 1), F32), pltpu.VMEM((tq, 1), F32),
                        pltpu.VMEM((tq, HEAD_DIM), F32)],
        compiler_params=_params("arbitrary", "arbitrary", "arbitrary"),
        name="fox_attention",
    )(q, kv, kv, f_cum.reshape(nh, 1, n))


def kernel(x, c, mod_w, mod_b, norm_mix_g, norm_ffn_g, ssm_w_in, ssm_log_step, ssm_a_re, ssm_a_im,
           ssm_b_re, ssm_b_im, ssm_c_re, ssm_c_im, ssm_d, ssm_glu_w, ssm_glu_b, ssm_w_out,
           kv_mod_w, kv_mod_b, kv_norm_g, kv_w, forget_b, attn_w_q, attn_w_out,
           ffn_w_up, ffn_conv_w, ffn_conv_b, ffn_w_down, final_norm_g):
    bsz, seq, d = x.shape
    n = bsz * seq
    depth = mod_w.shape[0]
    n_a = ssm_w_in.shape[0]
    assert n_a >= 1 and d % LANES == 0
    x2 = x.astype(F32).reshape(n, d)

    c_rep = jnp.broadcast_to(c.astype(F32)[:, :, None], (bsz, d, LANES))

    def vecs(m, count):
        return [m[:, k * d:(k + 1) * d].reshape(bsz, 1, d) for k in range(count)]

    mods = [vecs(_gemv(c_rep, mod_w[i], mod_b[i]), N_MOD) for i in range(depth)]
    sh_kv, sc_kv = vecs(_gemv(c_rep, kv_mod_w, kv_mod_b), 2)

    def mix_norm(i):
        return (norm_mix_g[i], mods[i][0], mods[i][1])

    def after_ffn(i):
        if i + 1 == depth:
            return ["plain"], [(final_norm_g,)]
        kinds, norms = [], []
        if i + 1 == n_a:
            kinds.append("mod")
            norms.append((kv_norm_g, sh_kv, sc_kv))
        kinds.append("mod")
        norms.append(mix_norm(i + 1))
        return kinds, norms

    h = _norm_mod(x2, *mix_norm(0), seq)
    kv = f_cum = None
    hk = flog_t = None
    out = None
    for i in range(depth):
        sh_m, sc_m, g_m, sh_f, sc_f, g_f = mods[i]
        ffn_norm = (norm_ffn_g[i], sh_f, sc_f)
        if i == n_a:
            kv = _mm(hk, kv_w[:, :2 * d].astype(BF16), name="kv_proj")
            f_cum = _fcum(flog_t, forget_b, bsz, seq)
        if i < n_a:
            u = _mm(h, ssm_w_in[i].astype(BF16), name="ssm_in_proj")
            bpack, cpack, coef = _s5_pack(ssm_log_step[i], ssm_a_re[i], ssm_a_im[i], ssm_b_re[i],
                                          ssm_b_im[i], ssm_c_re[i], ssm_c_im[i])
            g = _ssm(u, bpack, cpack, coef, ssm_d[i].astype(F32), bsz, seq)
            z = _mm(g, ssm_glu_w[i].astype(BF16), glu_bias=ssm_glu_b[i].astype(F32), name="ssm_glu")
            x2, h = _mm_res(z, ssm_w_out[i].astype(BF16), x2, g_m, ["mod"], [ffn_norm], seq,
                            name="ssm_out_proj")
        else:
            j = i - n_a
            q = _mm(h, attn_w_q[j].astype(BF16), out_scale=HEAD_DIM ** -0.5, name="q_proj")
            o = _attention(q, kv, f_cum, bsz, seq)
            x2, h = _mm_res(o, attn_w_out[j].astype(BF16), x2, g_m, ["mod"], [ffn_norm], seq,
                            name="attn_out_proj")
        kinds, norms = after_ffn(i)
        last = i + 1 == depth
        want_kv = (i + 1 == n_a) and not last
        wf_t = kv_w[:, 2 * d:].astype(F32).T if want_kv else None
        res = _ffn(h, x2, ffn_w_up[i].astype(BF16), ffn_conv_w[i], ffn_conv_b[i],
                   ffn_w_down[i].astype(BF16), g_f, kinds, norms, seq,
                   emit_x=not last, wf_t=wf_t, name="conv_ffn")
        if last:
            out = res[0]
        else:
            x2 = res[0]
            if want_kv:
                hk, h, flog_t = res[1], res[2], res[3]
            else:
                h = res[1]
    return out.reshape(bsz, seq, d).astype(x.dtype)
```

```python
import functools
import math

import jax
import jax.numpy as jnp
from jax import lax
from jax.experimental import pallas as pl
from jax.experimental.pallas import tpu as pltpu

F32 = jnp.float32
BF16 = jnp.bfloat16

LANES = 128
SUBLANES = 8
BF16_ROWS = 16
VMEM_LIMIT_BYTES = 56 * 1024 * 1024

SSM_GROUP = 16
SSM_STATE = 64
HEAD_DIM = 128
CONV_WIDTH = 3
NORM_EPS = 1e-6
LOG2E = math.log2(math.e)
N_MOD = 6

GROUPS_PER_BLOCK = LANES // SSM_GROUP
STATE_LANES = GROUPS_PER_BLOCK * SSM_STATE
SCAN_STEPS = (1, 2, 4)


def _params(*sem):
    return pltpu.CompilerParams(dimension_semantics=sem, vmem_limit_bytes=VMEM_LIMIT_BYTES)


def _tile(n, pref, align):
    if n <= pref:
        return n
    t = (pref // align) * align
    while t >= align:
        if n % t == 0:
            return t
        t -= align
    raise ValueError(f"no tile for {n}")


def _rms_hat(x):
    ms = jnp.mean(x * x, axis=-1, keepdims=True)
    return x * lax.rsqrt(ms + NORM_EPS)


def _modulate(y, shift, scale):
    return y * (1 + scale) + shift


def _gemv_kernel(c_ref, w_ref, b_ref, o_ref, cs_ref):
    @pl.when(pl.program_id(0) == 0)
    def _():
        cv = c_ref[...]
        cs_ref[...] = cv * jax.nn.sigmoid(cv)

    tn = w_ref.shape[1]
    for b in range(c_ref.shape[0]):
        cb = cs_ref[b]
        for lt in range(tn // LANES):
            sl = slice(lt * LANES, (lt + 1) * LANES)
            o_ref[b:b + 1, sl] = jnp.sum(w_ref[:, sl] * cb, axis=0, keepdims=True) + b_ref[:, sl]


def _gemv(c_rep, w_stack, layer, b):
    bsz, k, _ = c_rep.shape
    n = w_stack.shape[2]
    tn = _tile(n, 1024, LANES)
    return pl.pallas_call(
        _gemv_kernel,
        out_shape=jax.ShapeDtypeStruct((bsz, n), F32),
        grid=(n // tn,),
        in_specs=[pl.BlockSpec((bsz, k, LANES), lambda j: (0, 0, 0)),
                  pl.BlockSpec((None, k, tn), lambda j: (layer, 0, j)),
                  pl.BlockSpec((1, tn), lambda j: (0, j))],
        out_specs=pl.BlockSpec((bsz, tn), lambda j: (0, j)),
        scratch_shapes=[pltpu.VMEM((bsz, k, LANES), F32)],
        compiler_params=_params("arbitrary"),
        name="adaln_gemv",
    )(c_rep, w_stack, b.reshape(1, n))


def _norm_mod_kernel(x_ref, g_ref, sh_ref, sc_ref, o_ref):
    y = _rms_hat(x_ref[...]) * g_ref[...]
    o_ref[...] = _modulate(y, sh_ref[0], sc_ref[0]).astype(o_ref.dtype)


def _norm_mod(x2, g, shift, scale, seq):
    n, d = x2.shape
    tm = _tile(seq, 512, BF16_ROWS)
    tpb = seq // tm
    vec = pl.BlockSpec((1, 1, d), lambda i: (i // tpb, 0, 0))
    return pl.pallas_call(
        _norm_mod_kernel,
        out_shape=jax.ShapeDtypeStruct((n, d), BF16),
        grid=(n // tm,),
        in_specs=[pl.BlockSpec((tm, d), lambda i: (i, 0)),
                  pl.BlockSpec((1, d), lambda i: (0, 0)), vec, vec],
        out_specs=pl.BlockSpec((tm, d), lambda i: (i, 0)),
        compiler_params=_params("arbitrary"),
        name="norm_mod",
    )(x2, g.reshape(1, d), shift, scale)


def _mm_kernel(a_ref, w_ref, *rest, glu, out_scale):
    acc = jnp.dot(a_ref[...], w_ref[...], preferred_element_type=F32)
    if glu:
        g_ref, b_ref, o_ref = rest
        o_ref[...] = (g_ref[...].astype(F32) * jax.nn.sigmoid(acc + b_ref[...])).astype(o_ref.dtype)
    else:
        (o_ref,) = rest
        if out_scale != 1.0:
            acc = acc * out_scale
        o_ref[...] = acc.astype(o_ref.dtype)


def _mm(a, w, *, glu_bias=None, out_scale=1.0, name):
    n, k = a.shape
    nout = w.shape[1]
    tm = _tile(n, 1024, BF16_ROWS)
    tn = _tile(nout, 1024, LANES)
    in_specs = [pl.BlockSpec((tm, k), lambda j, i: (i, 0)),
                pl.BlockSpec((k, tn), lambda j, i: (0, j))]
    args = [a, w]
    if glu_bias is not None:
        in_specs += [pl.BlockSpec((tm, tn), lambda j, i: (i, j)),
                     pl.BlockSpec((1, tn), lambda j, i: (0, j))]
        args += [a, glu_bias.reshape(1, nout)]
    return pl.pallas_call(
        functools.partial(_mm_kernel, glu=glu_bias is not None, out_scale=out_scale),
        out_shape=jax.ShapeDtypeStruct((n, nout), BF16),
        grid=(nout // tn, n // tm),
        in_specs=in_specs,
        out_specs=pl.BlockSpec((tm, tn), lambda j, i: (i, j)),
        compiler_params=_params("arbitrary", "arbitrary"),
        name=name,
    )(*args)


def _norm_arg_count(kinds):
    return sum(3 if kd == "mod" else 1 for kd in kinds)


def _emit_norms(xn, kinds, norm_refs, out_refs):
    xhat = _rms_hat(xn)
    first = None
    p = 0
    for kd, o_ref in zip(kinds, out_refs):
        if kd == "mod":
            g_ref, sh_ref, sc_ref = norm_refs[p:p + 3]
            p += 3
            val = _modulate(xhat * g_ref[...], sh_ref[0], sc_ref[0])
        else:
            val = xhat * norm_refs[p][...]
            p += 1
        if first is None:
            first = val
        o_ref[...] = val.astype(o_ref.dtype)
    return first


def _norm_specs_args(kinds, norms, d, bidx):
    specs, args = [], []
    for kd, nm in zip(kinds, norms):
        specs.append(pl.BlockSpec((1, d), lambda *ix: (0, 0)))
        args.append(nm[0].reshape(1, d))
        if kd == "mod":
            for v in nm[1:]:
                specs.append(pl.BlockSpec((1, 1, d), bidx))
                args.append(v)
    return specs, args


def _norm_outs(kinds, n, d, tm, row_map):
    shapes = [jax.ShapeDtypeStruct((n, d), BF16 if kd == "mod" else F32) for kd in kinds]
    specs = [pl.BlockSpec((tm, d), row_map) for _ in kinds]
    return shapes, specs


def _mm_res_kernel(a_ref, w_ref, x_ref, gate_ref, *rest, kinds):
    na = _norm_arg_count(kinds)
    norm_refs, outs = rest[:na], rest[na:]
    acc = jnp.dot(a_ref[...], w_ref[...], preferred_element_type=F32)
    xn = x_ref[...] + gate_ref[0] * acc
    outs[0][...] = xn
    _emit_norms(xn, kinds, norm_refs, outs[1:])


def _mm_res(a, w, x2, gate, kinds, norms, seq, *, name):
    n, k = a.shape
    d = w.shape[1]
    tm = _tile(seq, 512, BF16_ROWS)
    tpb = seq // tm
    bidx = lambda i: (i // tpb, 0, 0)
    row = lambda i: (i, 0)
    nspecs, nargs = _norm_specs_args(kinds, norms, d, bidx)
    oshapes, ospecs = _norm_outs(kinds, n, d, tm, row)
    return pl.pallas_call(
        functools.partial(_mm_res_kernel, kinds=tuple(kinds)),
        out_shape=[jax.ShapeDtypeStruct((n, d), F32)] + oshapes,
        grid=(n // tm,),
        in_specs=[pl.BlockSpec((tm, k), row),
                  pl.BlockSpec((k, d), lambda i: (0, 0)),
                  pl.BlockSpec((tm, d), row),
                  pl.BlockSpec((1, 1, d), bidx)] + nspecs,
        out_specs=[pl.BlockSpec((tm, d), row)] + ospecs,
        compiler_params=_params("arbitrary"),
        name=name,
    )(a, w, x2, gate, *nargs)


def _ssm_kernel(u_ref, bp_ref, cp_ref, coef_ref, d_ref, o_ref, st_ref, carry_ref):
    s = STATE_LANES
    tt = u_ref.shape[0]

    @pl.when(pl.program_id(2) == 0)
    def _():
        carry_ref[...] = jnp.zeros_like(carry_ref)

    u = u_ref[...]
    st_ref[...] = jnp.dot(u, bp_ref[0], preferred_element_type=F32)
    steps = [(dlt, coef_ref[0, 2 * q], coef_ref[0, 2 * q + 1]) for q, dlt in enumerate(SCAN_STEPS)]
    pr = coef_ref[0, 2 * len(SCAN_STEPS)]
    pi = coef_ref[0, 2 * len(SCAN_STEPS) + 1]

    def body(i, carry):
        cr, ci = carry
        r0 = pl.multiple_of(i * SUBLANES, SUBLANES)
        xr = st_ref[pl.ds(r0, SUBLANES), 0:s]
        xi = st_ref[pl.ds(r0, SUBLANES), s:2 * s]
        for dlt, ar, ai in steps:
            sr = pltpu.roll(xr, dlt, 0)
            si = pltpu.roll(xi, dlt, 0)
            xr, xi = xr + (ar * sr - ai * si), xi + (ar * si + ai * sr)
        xr, xi = xr + (pr * cr - pi * ci), xi + (pr * ci + pi * cr)
        st_ref[pl.ds(r0, SUBLANES), 0:s] = xr
        st_ref[pl.ds(r0, SUBLANES), s:2 * s] = xi
        last = SUBLANES - 1
        return (jnp.broadcast_to(xr[last:last + 1], xr.shape),
                jnp.broadcast_to(xi[last:last + 1], xi.shape))

    cr, ci = lax.fori_loop(0, tt // SUBLANES, body, (carry_ref[0], carry_ref[1]), unroll=2)
    carry_ref[0] = cr
    carry_ref[1] = ci
    y = jnp.dot(st_ref[...].astype(BF16), cp_ref[0], preferred_element_type=F32)
    y = y + d_ref[...] * u.astype(F32)
    o_ref[...] = jax.nn.gelu(y).astype(o_ref.dtype)


def _s5_pack(log_step, a_re, a_im, b_re, b_im, c_re, c_im):
    g, p = a_re.shape
    h = b_re.shape[-1]
    nblk = g // GROUPS_PER_BLOCK
    step = jnp.exp(log_step.astype(F32))[:, None]
    lam_re, lam_im = a_re.astype(F32), a_im.astype(F32)

    def power(k):
        mag = jnp.exp(k * lam_re * step)
        return mag * jnp.cos(k * lam_im * step), mag * jnp.sin(k * lam_im * step)

    abar_re, abar_im = power(1)
    den = lam_re * lam_re + lam_im * lam_im
    nr = abar_re - 1.0
    fr = (nr * lam_re + abar_im * lam_im) / den
    fi = (abar_im * lam_re - nr * lam_im) / den
    br, bi = b_re.astype(F32), b_im.astype(F32)
    bbar_re = fr[..., None] * br - fi[..., None] * bi
    bbar_im = fr[..., None] * bi + fi[..., None] * br
    eye = jnp.eye(GROUPS_PER_BLOCK, dtype=F32)

    def pack_b(bb):
        t = bb.reshape(nblk, GROUPS_PER_BLOCK, p, h)
        t = jnp.einsum("jgph,gk->jghkp", t, eye)
        return t.reshape(nblk, LANES, STATE_LANES)

    def pack_c(cc):
        t = cc.astype(F32).reshape(nblk, GROUPS_PER_BLOCK, h, p)
        t = jnp.einsum("jghp,gk->jgpkh", t, eye)
        return t.reshape(nblk, STATE_LANES, LANES)

    bpack = jnp.concatenate([pack_b(bbar_re), pack_b(bbar_im)], axis=-1).astype(BF16)
    cpack = jnp.concatenate([pack_c(c_re), -pack_c(c_im)], axis=1).astype(BF16)

    rows = jnp.arange(SUBLANES)

    def lanes(v):
        return v.reshape(nblk, 1, STATE_LANES)

    coefs = []
    for dlt in SCAN_STEPS:
        keep = (rows >= dlt).astype(F32)[None, :, None]
        pre, pim = power(float(dlt))
        coefs += [lanes(pre) * keep, lanes(pim) * keep]
    pw = [power(float(r + 1)) for r in range(SUBLANES)]
    coefs.append(jnp.stack([lanes(q[0])[:, 0] for q in pw], axis=1))
    coefs.append(jnp.stack([lanes(q[1])[:, 0] for q in pw], axis=1))
    coef = jnp.stack(coefs, axis=1)
    return bpack, cpack, coef


def _ssm(u, bpack, cpack, coef, dvec, bsz, seq):
    n, d = u.shape
    nblk = d // LANES
    tt = _tile(seq, 1024, BF16_ROWS)
    nt = seq // tt
    row = lambda b, j, t: (b * nt + t, j)
    return pl.pallas_call(
        _ssm_kernel,
        out_shape=jax.ShapeDtypeStruct((n, d), BF16),
        grid=(bsz, nblk, nt),
        in_specs=[pl.BlockSpec((tt, LANES), row),
                  pl.BlockSpec((1, LANES, 2 * STATE_LANES), lambda b, j, t: (j, 0, 0)),
                  pl.BlockSpec((1, 2 * STATE_LANES, LANES), lambda b, j, t: (j, 0, 0)),
                  pl.BlockSpec((1, 2 * len(SCAN_STEPS) + 2, SUBLANES, STATE_LANES),
                               lambda b, j, t: (j, 0, 0, 0)),
                  pl.BlockSpec((1, LANES), lambda b, j, t: (0, j))],
        out_specs=pl.BlockSpec((tt, LANES), row),
        scratch_shapes=[pltpu.VMEM((tt, 2 * STATE_LANES), F32),
                        pltpu.VMEM((2, SUBLANES, STATE_LANES), F32)],
        compiler_params=_params("arbitrary", "arbitrary", "arbitrary"),
        name="s5_ssm",
    )(u, bpack, cpack, coef, dvec.reshape(1, d))


def _ffn_kernel(h_ref, halo_ref, wu_ref, wv_ref, cwu_ref, cwv_ref, cbu_ref, cbv_ref, wd_ref,
                x_ref, gate_ref, *rest, kinds, emit_x, with_flog, tiles_per_batch):
    na = _norm_arg_count(kinds)
    norm_refs = rest[:na]
    rest = rest[na:]
    if with_flog:
        wf_ref, rest = rest[0], rest[1:]
    n_out = len(kinds) + int(emit_x) + int(with_flog)
    outs, (hx_ref, act_ref) = rest[:n_out], rest[n_out:]
    acc_ref = outs[0]
    i, f = pl.program_id(0), pl.program_id(1)
    nf = pl.num_programs(1) - 1
    halo = halo_ref.shape[0]

    @pl.when(f == 0)
    def _():
        first = (i % tiles_per_batch) == 0
        hv = halo_ref[...]
        hx_ref[0:halo] = jnp.where(first, jnp.zeros_like(hv), hv)
        hx_ref[halo:] = h_ref[...]
        acc_ref[...] = jnp.zeros_like(acc_ref)
        act_ref[...] = jnp.zeros_like(act_ref)

    def down_previous():
        acc_ref[...] += jnp.dot(act_ref[(f + 1) % 2], wd_ref[...], preferred_element_type=F32)

    @pl.when(f < nf)
    def _():
        hx = hx_ref[...]

        def conv(w_ref, cw_ref, cb_ref):
            a = jnp.dot(hx, w_ref[...], preferred_element_type=F32)
            cw = cw_ref[...]
            out = cw[CONV_WIDTH - 1:CONV_WIDTH] * a[halo:]
            for lag in range(1, CONV_WIDTH):
                tap = CONV_WIDTH - 1 - lag
                out = out + cw[tap:tap + 1] * pltpu.roll(a, lag, 0)[halo:]
            return out + cb_ref[...]

        cu = conv(wu_ref, cwu_ref, cbu_ref)
        cv = conv(wv_ref, cwv_ref, cbv_ref)
        down_previous()
        act_ref[f % 2] = (cu * jax.nn.sigmoid(cu) * cv).astype(BF16)

    @pl.when(f == nf)
    def _():
        down_previous()
        xn = x_ref[...] + gate_ref[0] * acc_ref[...]
        k = 0
        if emit_x:
            outs[0][...] = xn
            k = 1
        first = _emit_norms(xn, kinds, norm_refs, outs[k:k + len(kinds)])
        if with_flog:
            hi = first.astype(BF16)
            lo = (first - hi.astype(F32)).astype(BF16)
            wf = wf_ref[...]
            whi = wf.astype(BF16)
            wlo = (wf - whi.astype(F32)).astype(BF16)
            nt = (((1,), (1,)), ((), ()))
            nh = wf.shape[0]
            both = lax.dot_general(jnp.concatenate([whi, wlo], axis=0), hi, nt,
                                   preferred_element_type=F32)
            outs[-1][...] = (both[:nh] + both[nh:]
                             + lax.dot_general(whi, lo, nt, preferred_element_type=F32))


def _ffn(h, x2, w_up, conv_w, conv_b, w_down, gate, kinds, norms, seq, *, emit_x, wf_t=None, name):
    n, d = h.shape
    ff = w_down.shape[0]
    tm = _tile(seq, 512, BF16_ROWS)
    tf = _tile(ff, 512, LANES)
    nf = ff // tf
    tpb = seq // tm
    halo = BF16_ROWS
    hpt = tm // halo
    row = lambda i, f: (i, 0)
    bidx = lambda i, f: (i // tpb, 0, 0)
    up = lambda f: jnp.minimum(f, nf - 1)
    nspecs, nargs = _norm_specs_args(kinds, norms, d, bidx)
    oshapes, ospecs = _norm_outs(kinds, n, d, tm, row)
    if emit_x:
        oshapes = [jax.ShapeDtypeStruct((n, d), F32)] + oshapes
        ospecs = [pl.BlockSpec((tm, d), row)] + ospecs
    assert oshapes[0].dtype == F32
    extra_specs, extra_args = [], []
    if wf_t is not None:
        nh = wf_t.shape[0]
        extra_specs = [pl.BlockSpec((nh, d), lambda i, f: (0, 0))]
        extra_args = [wf_t]
        oshapes = oshapes + [jax.ShapeDtypeStruct((nh, n), F32)]
        ospecs = ospecs + [pl.BlockSpec((nh, tm), lambda i, f: (0, i))]
    cw = conv_w.astype(F32)
    cb = conv_b.astype(F32).reshape(1, 2 * ff)
    return pl.pallas_call(
        functools.partial(_ffn_kernel, kinds=tuple(kinds), emit_x=emit_x,
                          with_flog=wf_t is not None, tiles_per_batch=tpb),
        out_shape=oshapes,
        grid=(n // tm, nf + 1),
        in_specs=[pl.BlockSpec((tm, d), row),
                  pl.BlockSpec((halo, d), lambda i, f: (jnp.maximum(i * hpt - 1, 0), 0)),
                  pl.BlockSpec((d, tf), lambda i, f: (0, up(f))),
                  pl.BlockSpec((d, tf), lambda i, f: (0, nf + up(f))),
                  pl.BlockSpec((CONV_WIDTH, tf), lambda i, f: (0, up(f))),
                  pl.BlockSpec((CONV_WIDTH, tf), lambda i, f: (0, nf + up(f))),
                  pl.BlockSpec((1, tf), lambda i, f: (0, up(f))),
                  pl.BlockSpec((1, tf), lambda i, f: (0, nf + up(f))),
                  pl.BlockSpec((tf, d), lambda i, f: (jnp.maximum(f - 1, 0), 0)),
                  pl.BlockSpec((tm, d), row),
                  pl.BlockSpec((1, 1, d), bidx)] + nspecs + extra_specs,
        out_specs=ospecs,
        scratch_shapes=[pltpu.VMEM((tm + halo, d), BF16), pltpu.VMEM((2, tm, tf), BF16)],
        compiler_params=_params("arbitrary", "arbitrary"),
        name=name,
    )(h, h, w_up, w_up, cw, cw, cb, cb, w_down, x2, gate, *nargs, *extra_args)


def _fcum_kernel(z_ref, b_ref, tri_ref, o_ref):
    z = z_ref[...] + b_ref[...]
    lf = jnp.minimum(z, 0.0) - jnp.log1p(jnp.exp(-jnp.abs(z)))
    carry = jnp.zeros((z.shape[0], 1), F32)
    for cidx in range(z.shape[1] // LANES):
        sl = slice(cidx * LANES, (cidx + 1) * LANES)
        cs = jnp.dot(lf[:, sl], tri_ref[...], precision=lax.Precision.HIGHEST,
                     preferred_element_type=F32) + carry
        o_ref[:, sl] = cs * LOG2E
        carry = cs[:, LANES - 1:LANES]


def _fcum(flog_t, forget_b, bsz, seq):
    nh, n = flog_t.shape
    tri = (jnp.arange(LANES)[:, None] <= jnp.arange(LANES)[None, :]).astype(F32)
    return pl.pallas_call(
        _fcum_kernel,
        out_shape=jax.ShapeDtypeStruct((nh, n), F32),
        grid=(bsz,),
        in_specs=[pl.BlockSpec((nh, seq), lambda b: (0, b)),
                  pl.BlockSpec((nh, 1), lambda b: (0, 0)),
                  pl.BlockSpec((LANES, LANES), lambda b: (0, 0))],
        out_specs=pl.BlockSpec((nh, seq), lambda b: (0, b)),
        compiler_params=_params("arbitrary"),
        name="forget_cumsum",
    )(flog_t, forget_b.astype(F32).reshape(nh, 1), tri)


def _attn_kernel(q_ref, k_ref, v_ref, f_ref, o_ref, m_ref, l_ref, acc_ref, *, tk, heads):
    tq = q_ref.shape[0]
    qi = pl.program_id(2)
    m_ref[...] = jnp.full_like(m_ref, -jnp.inf)
    l_ref[...] = jnp.zeros_like(l_ref)
    acc_ref[...] = jnp.zeros_like(acc_ref)
    ncol = tk // LANES

    def head_step(hd, cidx, masked):
        hs = slice(hd * HEAD_DIM, (hd + 1) * HEAD_DIM)
        k0 = pl.multiple_of(cidx * tk, tk)
        kc = k_ref[pl.ds(k0, tk), hs]
        vc = v_ref[pl.ds(k0, tk), hs]
        s = lax.dot_general(q_ref[:, hs], kc, (((1,), (1,)), ((), ())), preferred_element_type=F32)
        s = s - f_ref[hd, :, pl.ds(k0, tk)]
        if masked:
            rows = qi * tq + lax.broadcasted_iota(jnp.int32, (tq, tk), 0)
            cols = k0 + lax.broadcasted_iota(jnp.int32, (tq, tk), 1)
            s = jnp.where(cols <= rows, s, -jnp.inf)
        cols_s = [s[:, j * LANES:(j + 1) * LANES] for j in range(ncol)]
        mc = cols_s[0]
        for sj in cols_s[1:]:
            mc = jnp.maximum(mc, sj)
        m_old = m_ref[hd]
        m_new = jnp.maximum(m_old, jnp.max(mc, axis=-1, keepdims=True))
        alpha = jnp.exp2(m_old - m_new)
        ps = [jnp.exp2(sj - m_new) for sj in cols_s]
        lsum = ps[0]
        for pj in ps[1:]:
            lsum = lsum + pj
        l_ref[hd] = alpha * l_ref[hd] + lsum
        p = jnp.concatenate(ps, axis=-1).astype(BF16)
        acc_ref[hd] = alpha * acc_ref[hd] + jnp.dot(p, vc, preferred_element_type=F32)
        m_ref[hd] = m_new

    def step(cidx, masked):
        for hd in range(heads):
            head_step(hd, cidx, masked)

    n_diag = tq // tk
    n_full = qi * n_diag

    def full_body(cidx, carry):
        step(cidx, False)
        return carry

    lax.fori_loop(0, n_full, full_body, 0)
    for dg in range(n_diag):
        step(n_full + dg, True)
    for hd in range(heads):
        l_row = jnp.sum(l_ref[hd], axis=-1, keepdims=True)
        o_ref[:, hd * HEAD_DIM:(hd + 1) * HEAD_DIM] = (acc_ref[hd] / l_row).astype(o_ref.dtype)


def _attention(q, kv, f_cum, bsz, seq):
    n, d = q.shape
    nh = d // HEAD_DIM
    heads = 2 if nh % 2 == 0 else 1
    tq = _tile(seq, 512, LANES)
    tk = tq
    nq = seq // tq
    width = heads * HEAD_DIM
    ngrp = nh // heads
    return pl.pallas_call(
        functools.partial(_attn_kernel, tk=tk, heads=heads),
        out_shape=jax.ShapeDtypeStruct((n, d), BF16),
        grid=(bsz, ngrp, nq),
        in_specs=[pl.BlockSpec((tq, width), lambda b, h, i: (b * nq + i, h)),
                  pl.BlockSpec((seq, width), lambda b, h, i: (b, h)),
                  pl.BlockSpec((seq, width), lambda b, h, i: (b, ngrp + h)),
                  pl.BlockSpec((heads, 1, seq), lambda b, h, i: (h, 0, b))],
        out_specs=pl.BlockSpec((tq, width), lambda b, h, i: (b * nq + i, h)),
        scratch_shapes=[pltpu.VMEM((heads, tq, LANES), F32), pltpu.VMEM((heads, tq, LANES), F32),
                        pltpu.VMEM((heads, tq, HEAD_DIM), F32)],
        compiler_params=_params("arbitrary", "arbitrary", "arbitrary"),
        name="fox_attention",
    )(q, kv, kv, f_cum.reshape(nh, 1, n))


def kernel(x, c, mod_w, mod_b, norm_mix_g, norm_ffn_g, ssm_w_in, ssm_log_step, ssm_a_re, ssm_a_im,
           ssm_b_re, ssm_b_im, ssm_c_re, ssm_c_im, ssm_d, ssm_glu_w, ssm_glu_b, ssm_w_out,
           kv_mod_w, kv_mod_b, kv_norm_g, kv_w, forget_b, attn_w_q, attn_w_out,
           ffn_w_up, ffn_conv_w, ffn_conv_b, ffn_w_down, final_norm_g):
    bsz, seq, d = x.shape
    n = bsz * seq
    depth = mod_w.shape[0]
    n_a = ssm_w_in.shape[0]
    assert n_a >= 1 and d % LANES == 0
    x2 = x.astype(F32).reshape(n, d)

    c_rep = jnp.broadcast_to(c.astype(F32)[:, :, None], (bsz, d, LANES))

    def vecs(m, count):
        return [m[:, k * d:(k + 1) * d].reshape(bsz, 1, d) for k in range(count)]

    mods = [vecs(_gemv(c_rep, mod_w, i, mod_b[i]), N_MOD) for i in range(depth)]
    sh_kv, sc_kv = vecs(_gemv(c_rep, kv_mod_w[None], 0, kv_mod_b), 2)

    def mix_norm(i):
        return (norm_mix_g[i], mods[i][0], mods[i][1])

    def after_ffn(i):
        if i + 1 == depth:
            return ["plain"], [(final_norm_g,)]
        kinds, norms = [], []
        if i + 1 == n_a:
            kinds.append("mod")
            norms.append((kv_norm_g, sh_kv, sc_kv))
        kinds.append("mod")
        norms.append(mix_norm(i + 1))
        return kinds, norms

    h = _norm_mod(x2, *mix_norm(0), seq)
    kv = f_cum = None
    hk = flog_t = None
    out = None
    for i in range(depth):
        sh_m, sc_m, g_m, sh_f, sc_f, g_f = mods[i]
        ffn_norm = (norm_ffn_g[i], sh_f, sc_f)
        if i == n_a:
            kv = _mm(hk, kv_w[:, :2 * d].astype(BF16), name="kv_proj")
            f_cum = _fcum(flog_t, forget_b, bsz, seq)
        if i < n_a:
            u = _mm(h, ssm_w_in[i].astype(BF16), name="ssm_in_proj")
            bpack, cpack, coef = _s5_pack(ssm_log_step[i], ssm_a_re[i], ssm_a_im[i], ssm_b_re[i],
                                          ssm_b_im[i], ssm_c_re[i], ssm_c_im[i])
            g = _ssm(u, bpack, cpack, coef, ssm_d[i].astype(F32), bsz, seq)
            z = _mm(g, ssm_glu_w[i].astype(BF16), glu_bias=ssm_glu_b[i].astype(F32), name="ssm_glu")
            x2, h = _mm_res(z, ssm_w_out[i].astype(BF16), x2, g_m, ["mod"], [ffn_norm], seq,
                            name="ssm_out_proj")
        else:
            j = i - n_a
            q = _mm(h, attn_w_q[j].astype(BF16), out_scale=HEAD_DIM ** -0.5 * LOG2E, name="q_proj")
            o = _attention(q, kv, f_cum, bsz, seq)
            x2, h = _mm_res(o, attn_w_out[j].astype(BF16), x2, g_m, ["mod"], [ffn_norm], seq,
                            name="attn_out_proj")
        kinds, norms = after_ffn(i)
        last = i + 1 == depth
        want_kv = (i + 1 == n_a) and not last
        wf_t = kv_w[:, 2 * d:].astype(F32).T if want_kv else None
        res = _ffn(h, x2, ffn_w_up[i].astype(BF16), ffn_conv_w[i], ffn_conv_b[i],
                   ffn_w_down[i].astype(BF16), g_f, kinds, norms, seq,
                   emit_x=not last, wf_t=wf_t, name="conv_ffn")
        if last:
            out = res[0]
        else:
            x2 = res[0]
            if want_kv:
                hk, h, flog_t = res[1], res[2], res[3]
            else:
                h = res[1]
    return out.reshape(bsz, seq, d).astype(x.dtype)
```

```python
import functools
import math

import jax
import jax.numpy as jnp
from jax import lax
from jax.experimental import pallas as pl
from jax.experimental.pallas import tpu as pltpu

F32 = jnp.float32
BF16 = jnp.bfloat16

LANES = 128
SUBLANES = 8
BF16_ROWS = 16
VMEM_LIMIT_BYTES = 56 * 1024 * 1024

SSM_GROUP = 16
SSM_STATE = 64
HEAD_DIM = 128
CONV_WIDTH = 3
NORM_EPS = 1e-6
LOG2E = math.log2(math.e)
N_MOD = 6

GROUPS_PER_BLOCK = LANES // SSM_GROUP
STATE_LANES = GROUPS_PER_BLOCK * SSM_STATE
SCAN_STEPS = (1, 2, 4)
SSM_CHUNK = 512


def _params(*sem):
    return pltpu.CompilerParams(dimension_semantics=sem, vmem_limit_bytes=VMEM_LIMIT_BYTES)


def _tile(n, pref, align):
    if n <= pref:
        return n
    t = (pref // align) * align
    while t >= align:
        if n % t == 0:
            return t
        t -= align
    raise ValueError(f"no tile for {n}")


def _rms_hat(x):
    ms = jnp.mean(x * x, axis=-1, keepdims=True)
    return x * lax.rsqrt(ms + NORM_EPS)


def _modulate(y, shift, scale):
    return y * (1 + scale) + shift


def _gemv_kernel(c_ref, w_ref, b_ref, o_ref, cs_ref):
    @pl.when(pl.program_id(0) == 0)
    def _():
        cv = c_ref[...]
        cs_ref[...] = cv * jax.nn.sigmoid(cv)

    tn = w_ref.shape[1]
    for b in range(c_ref.shape[0]):
        cb = cs_ref[b]
        for lt in range(tn // LANES):
            sl = slice(lt * LANES, (lt + 1) * LANES)
            o_ref[b:b + 1, sl] = jnp.sum(w_ref[:, sl] * cb, axis=0, keepdims=True) + b_ref[:, sl]


def _gemv(c_rep, w_stack, layer, b):
    bsz, k, _ = c_rep.shape
    n = w_stack.shape[2]
    tn = _tile(n, 1024, LANES)
    return pl.pallas_call(
        _gemv_kernel,
        out_shape=jax.ShapeDtypeStruct((bsz, n), F32),
        grid=(n // tn,),
        in_specs=[pl.BlockSpec((bsz, k, LANES), lambda j: (0, 0, 0)),
                  pl.BlockSpec((None, k, tn), lambda j: (layer, 0, j)),
                  pl.BlockSpec((1, tn), lambda j: (0, j))],
        out_specs=pl.BlockSpec((bsz, tn), lambda j: (0, j)),
        scratch_shapes=[pltpu.VMEM((bsz, k, LANES), F32)],
        compiler_params=_params("arbitrary"),
        name="adaln_gemv",
    )(c_rep, w_stack, b.reshape(1, n))


def _norm_mod_kernel(x_ref, g_ref, sh_ref, sc_ref, o_ref):
    y = _rms_hat(x_ref[...]) * g_ref[...]
    o_ref[...] = _modulate(y, sh_ref[0], sc_ref[0]).astype(o_ref.dtype)


def _norm_mod(x2, g, shift, scale, seq):
    n, d = x2.shape
    tm = _tile(seq, 512, BF16_ROWS)
    tpb = seq // tm
    vec = pl.BlockSpec((1, 1, d), lambda i: (i // tpb, 0, 0))
    return pl.pallas_call(
        _norm_mod_kernel,
        out_shape=jax.ShapeDtypeStruct((n, d), BF16),
        grid=(n // tm,),
        in_specs=[pl.BlockSpec((tm, d), lambda i: (i, 0)),
                  pl.BlockSpec((1, d), lambda i: (0, 0)), vec, vec],
        out_specs=pl.BlockSpec((tm, d), lambda i: (i, 0)),
        compiler_params=_params("arbitrary"),
        name="norm_mod",
    )(x2, g.reshape(1, d), shift, scale)


def _mm_kernel(a_ref, w_ref, *rest, glu, out_scale):
    acc = jnp.dot(a_ref[...], w_ref[...], preferred_element_type=F32)
    if glu:
        g_ref, b_ref, o_ref = rest
        o_ref[...] = (g_ref[...].astype(F32) * jax.nn.sigmoid(acc + b_ref[...])).astype(o_ref.dtype)
    else:
        (o_ref,) = rest
        if out_scale != 1.0:
            acc = acc * out_scale
        o_ref[...] = acc.astype(o_ref.dtype)


def _mm(a, w, *, glu_bias=None, out_scale=1.0, name):
    n, k = a.shape
    nout = w.shape[1]
    tm = _tile(n, 1024, BF16_ROWS)
    tn = _tile(nout, 1024, LANES)
    in_specs = [pl.BlockSpec((tm, k), lambda j, i: (i, 0)),
                pl.BlockSpec((k, tn), lambda j, i: (0, j))]
    args = [a, w]
    if glu_bias is not None:
        in_specs += [pl.BlockSpec((tm, tn), lambda j, i: (i, j)),
                     pl.BlockSpec((1, tn), lambda j, i: (0, j))]
        args += [a, glu_bias.reshape(1, nout)]
    return pl.pallas_call(
        functools.partial(_mm_kernel, glu=glu_bias is not None, out_scale=out_scale),
        out_shape=jax.ShapeDtypeStruct((n, nout), BF16),
        grid=(nout // tn, n // tm),
        in_specs=in_specs,
        out_specs=pl.BlockSpec((tm, tn), lambda j, i: (i, j)),
        compiler_params=_params("arbitrary", "arbitrary"),
        name=name,
    )(*args)


def _norm_arg_count(kinds):
    return sum(3 if kd == "mod" else 1 for kd in kinds)


def _emit_norms(xn, kinds, norm_refs, out_refs):
    xhat = _rms_hat(xn)
    first = None
    p = 0
    for kd, o_ref in zip(kinds, out_refs):
        if kd == "mod":
            g_ref, sh_ref, sc_ref = norm_refs[p:p + 3]
            p += 3
            val = _modulate(xhat * g_ref[...], sh_ref[0], sc_ref[0])
        else:
            val = xhat * norm_refs[p][...]
            p += 1
        if first is None:
            first = val
        o_ref[...] = val.astype(o_ref.dtype)
    return first


def _norm_specs_args(kinds, norms, d, bidx):
    specs, args = [], []
    for kd, nm in zip(kinds, norms):
        specs.append(pl.BlockSpec((1, d), lambda *ix: (0, 0)))
        args.append(nm[0].reshape(1, d))
        if kd == "mod":
            for v in nm[1:]:
                specs.append(pl.BlockSpec((1, 1, d), bidx))
                args.append(v)
    return specs, args


def _norm_outs(kinds, n, d, tm, row_map):
    shapes = [jax.ShapeDtypeStruct((n, d), BF16 if kd == "mod" else F32) for kd in kinds]
    specs = [pl.BlockSpec((tm, d), row_map) for _ in kinds]
    return shapes, specs


def _mm_res_kernel(a_ref, w_ref, x_ref, gate_ref, *rest, kinds):
    na = _norm_arg_count(kinds)
    norm_refs, outs = rest[:na], rest[na:]
    acc = jnp.dot(a_ref[...], w_ref[...], preferred_element_type=F32)
    xn = x_ref[...] + gate_ref[0] * acc
    outs[0][...] = xn
    _emit_norms(xn, kinds, norm_refs, outs[1:])


def _mm_res(a, w, x2, gate, kinds, norms, seq, *, name):
    n, k = a.shape
    d = w.shape[1]
    tm = _tile(seq, 512, BF16_ROWS)
    tpb = seq // tm
    bidx = lambda i: (i // tpb, 0, 0)
    row = lambda i: (i, 0)
    nspecs, nargs = _norm_specs_args(kinds, norms, d, bidx)
    oshapes, ospecs = _norm_outs(kinds, n, d, tm, row)
    return pl.pallas_call(
        functools.partial(_mm_res_kernel, kinds=tuple(kinds)),
        out_shape=[jax.ShapeDtypeStruct((n, d), F32)] + oshapes,
        grid=(n // tm,),
        in_specs=[pl.BlockSpec((tm, k), row),
                  pl.BlockSpec((k, d), lambda i: (0, 0)),
                  pl.BlockSpec((tm, d), row),
                  pl.BlockSpec((1, 1, d), bidx)] + nspecs,
        out_specs=[pl.BlockSpec((tm, d), row)] + ospecs,
        compiler_params=_params("arbitrary"),
        name=name,
    )(a, w, x2, gate, *nargs)


def _ssm_kernel(u_ref, bp_ref, cp_ref, coef_ref, d_ref, o_ref, st_ref, carry_ref, *, slabs):
    s = STATE_LANES
    tt = u_ref.shape[0]

    @pl.when(pl.program_id(2) == 0)
    def _():
        carry_ref[...] = jnp.zeros_like(carry_ref)

    for sb in range(slabs):
        ch = slice(sb * LANES, (sb + 1) * LANES)
        u = u_ref[:, ch]
        st_ref[sb] = jnp.dot(u, bp_ref[sb], preferred_element_type=F32)
        steps = [(dlt, coef_ref[sb, 2 * q], coef_ref[sb, 2 * q + 1])
                 for q, dlt in enumerate(SCAN_STEPS)]
        pr = coef_ref[sb, 2 * len(SCAN_STEPS)]
        pi = coef_ref[sb, 2 * len(SCAN_STEPS) + 1]
        cr, ci = carry_ref[sb, 0], carry_ref[sb, 1]
        for i in range(tt // SUBLANES):
            rows = slice(i * SUBLANES, (i + 1) * SUBLANES)
            xr = st_ref[sb, rows, 0:s]
            xi = st_ref[sb, rows, s:2 * s]
            for dlt, ar, ai in steps:
                sr = pltpu.roll(xr, dlt, 0)
                si = pltpu.roll(xi, dlt, 0)
                xr, xi = xr + (ar * sr - ai * si), xi + (ar * si + ai * sr)
            xr, xi = xr + (pr * cr - pi * ci), xi + (pr * ci + pi * cr)
            st_ref[sb, rows, 0:s] = xr
            st_ref[sb, rows, s:2 * s] = xi
            last = SUBLANES - 1
            cr = jnp.broadcast_to(xr[last:last + 1], xr.shape)
            ci = jnp.broadcast_to(xi[last:last + 1], xi.shape)
        carry_ref[sb, 0] = cr
        carry_ref[sb, 1] = ci
        y = jnp.dot(st_ref[sb].astype(BF16), cp_ref[sb], preferred_element_type=F32)
        y = y + d_ref[:, ch] * u.astype(F32)
        o_ref[:, ch] = jax.nn.gelu(y).astype(o_ref.dtype)


def _s5_pack(log_step, a_re, a_im, b_re, b_im, c_re, c_im):
    g, p = a_re.shape
    h = b_re.shape[-1]
    nblk = g // GROUPS_PER_BLOCK
    step = jnp.exp(log_step.astype(F32))[:, None]
    lam_re, lam_im = a_re.astype(F32), a_im.astype(F32)

    def power(k):
        mag = jnp.exp(k * lam_re * step)
        return mag * jnp.cos(k * lam_im * step), mag * jnp.sin(k * lam_im * step)

    abar_re, abar_im = power(1)
    den = lam_re * lam_re + lam_im * lam_im
    nr = abar_re - 1.0
    fr = (nr * lam_re + abar_im * lam_im) / den
    fi = (abar_im * lam_re - nr * lam_im) / den
    br, bi = b_re.astype(F32), b_im.astype(F32)
    bbar_re = fr[..., None] * br - fi[..., None] * bi
    bbar_im = fr[..., None] * bi + fi[..., None] * br
    eye = jnp.eye(GROUPS_PER_BLOCK, dtype=F32)

    def pack_b(bb):
        t = bb.reshape(nblk, GROUPS_PER_BLOCK, p, h)
        t = jnp.einsum("jgph,gk->jghkp", t, eye)
        return t.reshape(nblk, LANES, STATE_LANES)

    def pack_c(cc):
        t = cc.astype(F32).reshape(nblk, GROUPS_PER_BLOCK, h, p)
        t = jnp.einsum("jghp,gk->jgpkh", t, eye)
        return t.reshape(nblk, STATE_LANES, LANES)

    bpack = jnp.concatenate([pack_b(bbar_re), pack_b(bbar_im)], axis=-1).astype(BF16)
    cpack = jnp.concatenate([pack_c(c_re), -pack_c(c_im)], axis=1).astype(BF16)

    rows = jnp.arange(SUBLANES)

    def lanes(v):
        return v.reshape(nblk, 1, STATE_LANES)

    coefs = []
    for dlt in SCAN_STEPS:
        keep = (rows >= dlt).astype(F32)[None, :, None]
        pre, pim = power(float(dlt))
        coefs += [lanes(pre) * keep, lanes(pim) * keep]
    pw = [power(float(r + 1)) for r in range(SUBLANES)]
    coefs.append(jnp.stack([lanes(q[0])[:, 0] for q in pw], axis=1))
    coefs.append(jnp.stack([lanes(q[1])[:, 0] for q in pw], axis=1))
    coef = jnp.stack(coefs, axis=1)
    return bpack, cpack, coef


def _ssm(u, bpack, cpack, coef, dvec, bsz, seq):
    n, d = u.shape
    nblk = d // LANES
    tt = _tile(seq, SSM_CHUNK, BF16_ROWS)
    nt = seq // tt
    slabs = 2 if nblk % 2 == 0 else 1
    width = slabs * LANES
    row = lambda b, j, t: (b * nt + t, j)
    return pl.pallas_call(
        functools.partial(_ssm_kernel, slabs=slabs),
        out_shape=jax.ShapeDtypeStruct((n, d), BF16),
        grid=(bsz, nblk // slabs, nt),
        in_specs=[pl.BlockSpec((tt, width), row),
                  pl.BlockSpec((slabs, LANES, 2 * STATE_LANES), lambda b, j, t: (j, 0, 0)),
                  pl.BlockSpec((slabs, 2 * STATE_LANES, LANES), lambda b, j, t: (j, 0, 0)),
                  pl.BlockSpec((slabs, 2 * len(SCAN_STEPS) + 2, SUBLANES, STATE_LANES),
                               lambda b, j, t: (j, 0, 0, 0)),
                  pl.BlockSpec((1, width), lambda b, j, t: (0, j))],
        out_specs=pl.BlockSpec((tt, width), row),
        scratch_shapes=[pltpu.VMEM((slabs, tt, 2 * STATE_LANES), F32),
                        pltpu.VMEM((slabs, 2, SUBLANES, STATE_LANES), F32)],
        compiler_params=_params("arbitrary", "arbitrary", "arbitrary"),
        name="s5_ssm",
    )(u, bpack, cpack, coef, dvec.reshape(1, d))


def _ffn_up_kernel(h_ref, wu_ref, wv_ref, cwu_ref, cwv_ref, cbu_ref, cbv_ref, o_ref,
                   wub_ref, wvb_ref, carry_ref, *, tiles_per_batch, nsub):
    i = pl.program_id(1)

    @pl.when(i == 0)
    def _():
        wub_ref[...] = wu_ref[...].astype(BF16)
        wvb_ref[...] = wv_ref[...].astype(BF16)

    tm = h_ref.shape[0]
    ts = tm // nsub
    first = (i % tiles_per_batch) == 0
    prevs = []
    for slot in range(2):
        pv = carry_ref[slot]
        prevs.append(jnp.where(first, jnp.zeros_like(pv), pv))
    streams = ((wub_ref, cwu_ref, cbu_ref), (wvb_ref, cwv_ref, cbv_ref))
    for sb in range(nsub):
        rows = slice(sb * ts, (sb + 1) * ts)
        h = h_ref[rows, :]
        conv = []
        for slot, (wb_ref, cw_ref, cb_ref) in enumerate(streams):
            a = jnp.dot(h, wb_ref[...], preferred_element_type=F32)
            ext = jnp.concatenate([prevs[slot], a], axis=0)
            prevs[slot] = a[ts - SUBLANES:]
            cw = cw_ref[...]
            out = cw[CONV_WIDTH - 1:CONV_WIDTH] * a
            for lag in range(1, CONV_WIDTH):
                tap = CONV_WIDTH - 1 - lag
                out = out + cw[tap:tap + 1] * pltpu.roll(ext, lag, 0)[SUBLANES:]
            conv.append(out + cb_ref[...])
        cu, cv = conv
        o_ref[rows, :] = (cu * jax.nn.sigmoid(cu) * cv).astype(o_ref.dtype)
    for slot in range(2):
        carry_ref[slot] = prevs[slot]


def _ffn_up(h, w_up, conv_w, conv_b, layer, seq):
    n, d = h.shape
    ff = w_up.shape[2] // 2
    tm = _tile(seq, 1024, BF16_ROWS)
    tf = _tile(ff, 512, LANES)
    nf = ff // tf
    nsub = 2 if tm % (2 * BF16_ROWS) == 0 else 1
    cb = conv_b.astype(F32).reshape(conv_b.shape[0], 1, 2 * ff)
    return pl.pallas_call(
        functools.partial(_ffn_up_kernel, tiles_per_batch=seq // tm, nsub=nsub),
        out_shape=jax.ShapeDtypeStruct((n, ff), BF16),
        grid=(nf, n // tm),
        in_specs=[pl.BlockSpec((tm, d), lambda f, i: (i, 0)),
                  pl.BlockSpec((None, d, tf), lambda f, i: (layer, 0, f)),
                  pl.BlockSpec((None, d, tf), lambda f, i: (layer, 0, nf + f)),
                  pl.BlockSpec((None, CONV_WIDTH, tf), lambda f, i: (layer, 0, f)),
                  pl.BlockSpec((None, CONV_WIDTH, tf), lambda f, i: (layer, 0, nf + f)),
                  pl.BlockSpec((None, 1, tf), lambda f, i: (layer, 0, f)),
                  pl.BlockSpec((None, 1, tf), lambda f, i: (layer, 0, nf + f))],
        out_specs=pl.BlockSpec((tm, tf), lambda f, i: (i, f)),
        scratch_shapes=[pltpu.VMEM((d, tf), BF16), pltpu.VMEM((d, tf), BF16),
                        pltpu.VMEM((2, SUBLANES, tf), F32)],
        compiler_params=_params("arbitrary", "arbitrary"),
        name="ffn_up_conv",
    )(h, w_up, w_up, conv_w.astype(F32), conv_w.astype(F32), cb, cb)


def _ffn_down_kernel(a_ref, w_ref, x_ref, gate_ref, *rest, kinds, emit_x, with_flog):
    na = _norm_arg_count(kinds)
    norm_refs = rest[:na]
    rest = rest[na:]
    if with_flog:
        wf_ref, rest = rest[0], rest[1:]
    outs = rest
    acc_ref = outs[0]
    k = pl.program_id(1)
    contrib = jnp.dot(a_ref[...], w_ref[...], preferred_element_type=F32)

    @pl.when(k == 0)
    def _():
        acc_ref[...] = contrib

    @pl.when(k > 0)
    def _():
        acc_ref[...] += contrib

    @pl.when(k == pl.num_programs(1) - 1)
    def _():
        xn = x_ref[...] + gate_ref[0] * acc_ref[...]
        q = 0
        if emit_x:
            outs[0][...] = xn
            q = 1
        first = _emit_norms(xn, kinds, norm_refs, outs[q:q + len(kinds)])
        if with_flog:
            hi = first.astype(BF16)
            lo = (first - hi.astype(F32)).astype(BF16)
            wf = wf_ref[...]
            whi = wf.astype(BF16)
            wlo = (wf - whi.astype(F32)).astype(BF16)
            nt = (((1,), (1,)), ((), ()))
            nh = wf.shape[0]
            both = lax.dot_general(jnp.concatenate([whi, wlo], axis=0), hi, nt,
                                   preferred_element_type=F32)
            outs[-1][...] = (both[:nh] + both[nh:]
                             + lax.dot_general(whi, lo, nt, preferred_element_type=F32))


def _ffn_down(act, x2, w_down, gate, kinds, norms, seq, *, emit_x, wf_t=None):
    n, ff = act.shape
    d = w_down.shape[1]
    tm = _tile(seq, 512, BF16_ROWS)
    tk = _tile(ff, 1536, LANES)
    tpb = seq // tm
    row = lambda i, k: (i, 0)
    bidx = lambda i, k: (i // tpb, 0, 0)
    nspecs, nargs = _norm_specs_args(kinds, norms, d, bidx)
    oshapes, ospecs = _norm_outs(kinds, n, d, tm, row)
    if emit_x:
        oshapes = [jax.ShapeDtypeStruct((n, d), F32)] + oshapes
        ospecs = [pl.BlockSpec((tm, d), row)] + ospecs
    assert oshapes[0].dtype == F32
    extra_specs, extra_args = [], []
    if wf_t is not None:
        nh = wf_t.shape[0]
        extra_specs = [pl.BlockSpec((nh, d), lambda i, k: (0, 0))]
        extra_args = [wf_t]
        oshapes = oshapes + [jax.ShapeDtypeStruct((nh, n), F32)]
        ospecs = ospecs + [pl.BlockSpec((nh, tm), lambda i, k: (0, i))]
    return pl.pallas_call(
        functools.partial(_ffn_down_kernel, kinds=tuple(kinds), emit_x=emit_x,
                          with_flog=wf_t is not None),
        out_shape=oshapes,
        grid=(n // tm, ff // tk),
        in_specs=[pl.BlockSpec((tm, tk), lambda i, k: (i, k)),
                  pl.BlockSpec((tk, d), lambda i, k: (k, 0)),
                  pl.BlockSpec((tm, d), row),
                  pl.BlockSpec((1, 1, d), bidx)] + nspecs + extra_specs,
        out_specs=ospecs,
        compiler_params=_params("arbitrary", "arbitrary"),
        name="ffn_down_res",
    )(act, w_down, x2, gate, *nargs, *extra_args)


def _fcum_kernel(z_ref, b_ref, tri_ref, o_ref):
    z = z_ref[...] + b_ref[...]
    lf = jnp.minimum(z, 0.0) - jnp.log1p(jnp.exp(-jnp.abs(z)))
    carry = jnp.zeros((z.shape[0], 1), F32)
    for cidx in range(z.shape[1] // LANES):
        sl = slice(cidx * LANES, (cidx + 1) * LANES)
        cs = jnp.dot(lf[:, sl], tri_ref[...], precision=lax.Precision.HIGHEST,
                     preferred_element_type=F32) + carry
        o_ref[:, sl] = cs * LOG2E
        carry = cs[:, LANES - 1:LANES]


def _fcum(flog_t, forget_b, bsz, seq):
    nh, n = flog_t.shape
    tri = (jnp.arange(LANES)[:, None] <= jnp.arange(LANES)[None, :]).astype(F32)
    return pl.pallas_call(
        _fcum_kernel,
        out_shape=jax.ShapeDtypeStruct((nh, n), F32),
        grid=(bsz,),
        in_specs=[pl.BlockSpec((nh, seq), lambda b: (0, b)),
                  pl.BlockSpec((nh, 1), lambda b: (0, 0)),
                  pl.BlockSpec((LANES, LANES), lambda b: (0, 0))],
        out_specs=pl.BlockSpec((nh, seq), lambda b: (0, b)),
        compiler_params=_params("arbitrary"),
        name="forget_cumsum",
    )(flog_t, forget_b.astype(F32).reshape(nh, 1), tri)


def _attn_kernel(q_ref, k_ref, v_ref, f_ref, o_ref, m_ref, l_ref, acc_ref, *, tk, heads):
    tq = q_ref.shape[0]
    qi = pl.program_id(2)
    m_ref[...] = jnp.full_like(m_ref, -jnp.inf)
    l_ref[...] = jnp.zeros_like(l_ref)
    acc_ref[...] = jnp.zeros_like(acc_ref)
    ncol = tk // LANES

    def head_step(hd, cidx, masked):
        hs = slice(hd * HEAD_DIM, (hd + 1) * HEAD_DIM)
        k0 = pl.multiple_of(cidx * tk, tk)
        kc = k_ref[pl.ds(k0, tk), hs]
        vc = v_ref[pl.ds(k0, tk), hs]
        s = lax.dot_general(q_ref[:, hs], kc, (((1,), (1,)), ((), ())), preferred_element_type=F32)
        s = s - f_ref[hd, :, pl.ds(k0, tk)]
        if masked:
            rows = qi * tq + lax.broadcasted_iota(jnp.int32, (tq, tk), 0)
            cols = k0 + lax.broadcasted_iota(jnp.int32, (tq, tk), 1)
            s = jnp.where(cols <= rows, s, -jnp.inf)
        cols_s = [s[:, j * LANES:(j + 1) * LANES] for j in range(ncol)]
        mc = cols_s[0]
        for sj in cols_s[1:]:
            mc = jnp.maximum(mc, sj)
        m_old = m_ref[hd]
        m_new = jnp.maximum(m_old, jnp.max(mc, axis=-1, keepdims=True))
        alpha = jnp.exp2(m_old - m_new)
        ps = [jnp.exp2(sj - m_new) for sj in cols_s]
        lsum = ps[0]
        for pj in ps[1:]:
            lsum = lsum + pj
        l_ref[hd] = alpha * l_ref[hd] + lsum
        p = jnp.concatenate(ps, axis=-1).astype(BF16)
        acc_ref[hd] = alpha * acc_ref[hd] + jnp.dot(p, vc, preferred_element_type=F32)
        m_ref[hd] = m_new

    def step(cidx, masked):
        for hd in range(heads):
            head_step(hd, cidx, masked)

    n_diag = tq // tk
    n_full = qi * n_diag

    def full_body(cidx, carry):
        step(cidx, False)
        return carry

    lax.fori_loop(0, n_full, full_body, 0)
    for dg in range(n_diag):
        step(n_full + dg, True)
    for hd in range(heads):
        l_row = jnp.sum(l_ref[hd], axis=-1, keepdims=True)
        o_ref[:, hd * HEAD_DIM:(hd + 1) * HEAD_DIM] = (acc_ref[hd] / l_row).astype(o_ref.dtype)


def _attention(q, kv, f_cum, bsz, seq):
    n, d = q.shape
    nh = d // HEAD_DIM
    heads = 2 if nh % 2 == 0 else 1
    tq = _tile(seq, 512, LANES)
    tk = tq
    nq = seq // tq
    width = heads * HEAD_DIM
    ngrp = nh // heads
    return pl.pallas_call(
        functools.partial(_attn_kernel, tk=tk, heads=heads),
        out_shape=jax.ShapeDtypeStruct((n, d), BF16),
        grid=(bsz, ngrp, nq),
        in_specs=[pl.BlockSpec((tq, width), lambda b, h, i: (b * nq + i, h)),
                  pl.BlockSpec((seq, width), lambda b, h, i: (b, h)),
                  pl.BlockSpec((seq, width), lambda b, h, i: (b, ngrp + h)),
                  pl.BlockSpec((heads, 1, seq), lambda b, h, i: (h, 0, b))],
        out_specs=pl.BlockSpec((tq, width), lambda b, h, i: (b * nq + i, h)),
        scratch_shapes=[pltpu.VMEM((heads, tq, LANES), F32), pltpu.VMEM((heads, tq, LANES), F32),
                        pltpu.VMEM((heads, tq, HEAD_DIM), F32)],
        compiler_params=_params("arbitrary", "arbitrary", "arbitrary"),
        name="fox_attention",
    )(q, kv, kv, f_cum.reshape(nh, 1, n))


def kernel(x, c, mod_w, mod_b, norm_mix_g, norm_ffn_g, ssm_w_in, ssm_log_step, ssm_a_re, ssm_a_im,
           ssm_b_re, ssm_b_im, ssm_c_re, ssm_c_im, ssm_d, ssm_glu_w, ssm_glu_b, ssm_w_out,
           kv_mod_w, kv_mod_b, kv_norm_g, kv_w, forget_b, attn_w_q, attn_w_out,
           ffn_w_up, ffn_conv_w, ffn_conv_b, ffn_w_down, final_norm_g):
    bsz, seq, d = x.shape
    n = bsz * seq
    depth = mod_w.shape[0]
    n_a = ssm_w_in.shape[0]
    assert n_a >= 1 and d % LANES == 0
    x2 = x.astype(F32).reshape(n, d)

    c_rep = jnp.broadcast_to(c.astype(F32)[:, :, None], (bsz, d, LANES))

    def vecs(m, count):
        return [m[:, k * d:(k + 1) * d].reshape(bsz, 1, d) for k in range(count)]

    mods = [vecs(_gemv(c_rep, mod_w, i, mod_b[i]), N_MOD) for i in range(depth)]
    sh_kv, sc_kv = vecs(_gemv(c_rep, kv_mod_w[None], 0, kv_mod_b), 2)

    def mix_norm(i):
        return (norm_mix_g[i], mods[i][0], mods[i][1])

    def after_ffn(i):
        if i + 1 == depth:
            return ["plain"], [(final_norm_g,)]
        kinds, norms = [], []
        if i + 1 == n_a:
            kinds.append("mod")
            norms.append((kv_norm_g, sh_kv, sc_kv))
        kinds.append("mod")
        norms.append(mix_norm(i + 1))
        return kinds, norms

    h = _norm_mod(x2, *mix_norm(0), seq)
    kv = f_cum = None
    hk = flog_t = None
    out = None
    for i in range(depth):
        sh_m, sc_m, g_m, sh_f, sc_f, g_f = mods[i]
        ffn_norm = (norm_ffn_g[i], sh_f, sc_f)
        if i == n_a:
            kv = _mm(hk, kv_w[:, :2 * d].astype(BF16), name="kv_proj")
            f_cum = _fcum(flog_t, forget_b, bsz, seq)
        if i < n_a:
            u = _mm(h, ssm_w_in[i].astype(BF16), name="ssm_in_proj")
            bpack, cpack, coef = _s5_pack(ssm_log_step[i], ssm_a_re[i], ssm_a_im[i], ssm_b_re[i],
                                          ssm_b_im[i], ssm_c_re[i], ssm_c_im[i])
            g = _ssm(u, bpack, cpack, coef, ssm_d[i].astype(F32), bsz, seq)
            z = _mm(g, ssm_glu_w[i].astype(BF16), glu_bias=ssm_glu_b[i].astype(F32), name="ssm_glu")
            x2, h = _mm_res(z, ssm_w_out[i].astype(BF16), x2, g_m, ["mod"], [ffn_norm], seq,
                            name="ssm_out_proj")
        else:
            j = i - n_a
            q = _mm(h, attn_w_q[j].astype(BF16), out_scale=HEAD_DIM ** -0.5 * LOG2E, name="q_proj")
            o = _attention(q, kv, f_cum, bsz, seq)
            x2, h = _mm_res(o, attn_w_out[j].astype(BF16), x2, g_m, ["mod"], [ffn_norm], seq,
                            name="attn_out_proj")
        kinds, norms = after_ffn(i)
        last = i + 1 == depth
        want_kv = (i + 1 == n_a) and not last
        wf_t = kv_w[:, 2 * d:].astype(F32).T if want_kv else None
        act = _ffn_up(h, ffn_w_up, ffn_conv_w, ffn_conv_b, i, seq)
        res = _ffn_down(act, x2, ffn_w_down[i].astype(BF16), g_f, kinds, norms, seq,
                        emit_x=not last, wf_t=wf_t)
        if last:
            out = res[0]
        else:
            x2 = res[0]
            if want_kv:
                hk, h, flog_t = res[1], res[2], res[3]
            else:
                h = res[1]
    return out.reshape(bsz, seq, d).astype(x.dtype)
```

```python
import functools
import math

import jax
import jax.numpy as jnp
from jax import lax
from jax.experimental import pallas as pl
from jax.experimental.pallas import tpu as pltpu

F32 = jnp.float32
BF16 = jnp.bfloat16

LANES = 128
SUBLANES = 8
BF16_ROWS = 16
VMEM_LIMIT_BYTES = 56 * 1024 * 1024

SSM_GROUP = 16
SSM_STATE = 64
HEAD_DIM = 128
CONV_WIDTH = 3
NORM_EPS = 1e-6
LOG2E = math.log2(math.e)
N_MOD = 6

GROUPS_PER_BLOCK = LANES // SSM_GROUP
STATE_LANES = GROUPS_PER_BLOCK * SSM_STATE
SCAN_STEPS = (1, 2, 4)
SSM_CHUNK = 512
FFN_ROW_SPLITS = 4


def _params(*sem):
    return pltpu.CompilerParams(dimension_semantics=sem, vmem_limit_bytes=VMEM_LIMIT_BYTES)


def _tile(n, pref, align):
    if n <= pref:
        return n
    t = (pref // align) * align
    while t >= align:
        if n % t == 0:
            return t
        t -= align
    raise ValueError(f"no tile for {n}")


def _rms_hat(x):
    ms = jnp.mean(x * x, axis=-1, keepdims=True)
    return x * lax.rsqrt(ms + NORM_EPS)


def _modulate(y, shift, scale):
    return y * (1 + scale) + shift


def _gemv_kernel(c_ref, w_ref, b_ref, o_ref, cs_ref):
    @pl.when(pl.program_id(0) == 0)
    def _():
        cv = c_ref[...]
        cs_ref[...] = cv * jax.nn.sigmoid(cv)

    tn = w_ref.shape[1]
    for b in range(c_ref.shape[0]):
        cb = cs_ref[b]
        for lt in range(tn // LANES):
            sl = slice(lt * LANES, (lt + 1) * LANES)
            o_ref[b:b + 1, sl] = jnp.sum(w_ref[:, sl] * cb, axis=0, keepdims=True) + b_ref[:, sl]


def _gemv(c_rep, w_stack, layer, b):
    bsz, k, _ = c_rep.shape
    n = w_stack.shape[2]
    tn = _tile(n, 1024, LANES)
    return pl.pallas_call(
        _gemv_kernel,
        out_shape=jax.ShapeDtypeStruct((bsz, n), F32),
        grid=(n // tn,),
        in_specs=[pl.BlockSpec((bsz, k, LANES), lambda j: (0, 0, 0)),
                  pl.BlockSpec((None, k, tn), lambda j: (layer, 0, j)),
                  pl.BlockSpec((1, tn), lambda j: (0, j))],
        out_specs=pl.BlockSpec((bsz, tn), lambda j: (0, j)),
        scratch_shapes=[pltpu.VMEM((bsz, k, LANES), F32)],
        compiler_params=_params("arbitrary"),
        name="adaln_gemv",
    )(c_rep, w_stack, b.reshape(1, n))


def _norm_mod_kernel(x_ref, g_ref, sh_ref, sc_ref, o_ref):
    y = _rms_hat(x_ref[...]) * g_ref[...]
    o_ref[...] = _modulate(y, sh_ref[0], sc_ref[0]).astype(o_ref.dtype)


def _norm_mod(x2, g, shift, scale, seq):
    n, d = x2.shape
    tm = _tile(seq, 512, BF16_ROWS)
    tpb = seq // tm
    vec = pl.BlockSpec((1, 1, d), lambda i: (i // tpb, 0, 0))
    return pl.pallas_call(
        _norm_mod_kernel,
        out_shape=jax.ShapeDtypeStruct((n, d), BF16),
        grid=(n // tm,),
        in_specs=[pl.BlockSpec((tm, d), lambda i: (i, 0)),
                  pl.BlockSpec((1, d), lambda i: (0, 0)), vec, vec],
        out_specs=pl.BlockSpec((tm, d), lambda i: (i, 0)),
        compiler_params=_params("arbitrary"),
        name="norm_mod",
    )(x2, g.reshape(1, d), shift, scale)


def _mm_kernel(a_ref, w_ref, *rest, glu, out_scale):
    *rest, wb_ref = rest

    @pl.when(pl.program_id(1) == 0)
    def _():
        wb_ref[...] = w_ref[...].astype(BF16)

    acc = jnp.dot(a_ref[...], wb_ref[...], preferred_element_type=F32)
    if glu:
        g_ref, b_ref, o_ref = rest
        o_ref[...] = (g_ref[...].astype(F32) * jax.nn.sigmoid(acc + b_ref[...])).astype(o_ref.dtype)
    else:
        (o_ref,) = rest
        if out_scale != 1.0:
            acc = acc * out_scale
        o_ref[...] = acc.astype(o_ref.dtype)


def _mm(a, w_stack, layer, nout, *, glu_bias=None, out_scale=1.0, name):
    n, k = a.shape
    tm = _tile(n, 1024, BF16_ROWS)
    tn = _tile(nout, 1024, LANES)
    in_specs = [pl.BlockSpec((tm, k), lambda j, i: (i, 0)),
                pl.BlockSpec((None, k, tn), lambda j, i: (layer, 0, j))]
    args = [a, w_stack]
    if glu_bias is not None:
        in_specs += [pl.BlockSpec((tm, tn), lambda j, i: (i, j)),
                     pl.BlockSpec((1, tn), lambda j, i: (0, j))]
        args += [a, glu_bias.reshape(1, nout)]
    return pl.pallas_call(
        functools.partial(_mm_kernel, glu=glu_bias is not None, out_scale=out_scale),
        out_shape=jax.ShapeDtypeStruct((n, nout), BF16),
        grid=(nout // tn, n // tm),
        in_specs=in_specs,
        out_specs=pl.BlockSpec((tm, tn), lambda j, i: (i, j)),
        scratch_shapes=[pltpu.VMEM((k, tn), BF16)],
        compiler_params=_params("arbitrary", "arbitrary"),
        name=name,
    )(*args)


def _norm_arg_count(kinds):
    return sum(3 if kd == "mod" else 1 for kd in kinds)


def _emit_norms(xn, kinds, norm_refs, out_refs):
    xhat = _rms_hat(xn)
    first = None
    p = 0
    for kd, o_ref in zip(kinds, out_refs):
        if kd == "mod":
            g_ref, sh_ref, sc_ref = norm_refs[p:p + 3]
            p += 3
            val = _modulate(xhat * g_ref[...], sh_ref[0], sc_ref[0])
        else:
            val = xhat * norm_refs[p][...]
            p += 1
        if first is None:
            first = val
        o_ref[...] = val.astype(o_ref.dtype)
    return first


def _norm_specs_args(kinds, norms, d, bidx):
    specs, args = [], []
    for kd, nm in zip(kinds, norms):
        specs.append(pl.BlockSpec((1, d), lambda *ix: (0, 0)))
        args.append(nm[0].reshape(1, d))
        if kd == "mod":
            for v in nm[1:]:
                specs.append(pl.BlockSpec((1, 1, d), bidx))
                args.append(v)
    return specs, args


def _norm_outs(kinds, n, d, tm, row_map):
    shapes = [jax.ShapeDtypeStruct((n, d), BF16 if kd == "mod" else F32) for kd in kinds]
    specs = [pl.BlockSpec((tm, d), row_map) for _ in kinds]
    return shapes, specs


def _mm_res_kernel(a_ref, w_ref, x_ref, gate_ref, *rest, kinds):
    *rest, wb_ref = rest
    na = _norm_arg_count(kinds)
    norm_refs, outs = rest[:na], rest[na:]

    @pl.when(pl.program_id(0) == 0)
    def _():
        wb_ref[...] = w_ref[...].astype(BF16)

    acc = jnp.dot(a_ref[...], wb_ref[...], preferred_element_type=F32)
    xn = x_ref[...] + gate_ref[0] * acc
    outs[0][...] = xn
    _emit_norms(xn, kinds, norm_refs, outs[1:])


def _mm_res(a, w_stack, layer, x2, gate, kinds, norms, seq, *, name):
    n, k = a.shape
    d = w_stack.shape[2]
    tm = _tile(seq, 512, BF16_ROWS)
    tpb = seq // tm
    bidx = lambda i: (i // tpb, 0, 0)
    row = lambda i: (i, 0)
    nspecs, nargs = _norm_specs_args(kinds, norms, d, bidx)
    oshapes, ospecs = _norm_outs(kinds, n, d, tm, row)
    return pl.pallas_call(
        functools.partial(_mm_res_kernel, kinds=tuple(kinds)),
        out_shape=[jax.ShapeDtypeStruct((n, d), F32)] + oshapes,
        grid=(n // tm,),
        in_specs=[pl.BlockSpec((tm, k), row),
                  pl.BlockSpec((None, k, d), lambda i: (layer, 0, 0), pipeline_mode=pl.Buffered(1)),
                  pl.BlockSpec((tm, d), row),
                  pl.BlockSpec((1, 1, d), bidx)] + nspecs,
        out_specs=[pl.BlockSpec((tm, d), row)] + ospecs,
        scratch_shapes=[pltpu.VMEM((k, d), BF16)],
        compiler_params=_params("arbitrary"),
        name=name,
    )(a, w_stack, x2, gate, *nargs)


def _ssm_kernel(u_ref, bp_ref, cp_ref, coef_ref, d_ref, o_ref, st_ref, carry_ref, *, slabs):
    s = STATE_LANES
    tt = u_ref.shape[0]

    @pl.when(pl.program_id(2) == 0)
    def _():
        carry_ref[...] = jnp.zeros_like(carry_ref)

    for sb in range(slabs):
        ch = slice(sb * LANES, (sb + 1) * LANES)
        u = u_ref[:, ch]
        st_ref[sb] = jnp.dot(u, bp_ref[sb], preferred_element_type=F32)
        steps = [(dlt, coef_ref[sb, 2 * q], coef_ref[sb, 2 * q + 1])
                 for q, dlt in enumerate(SCAN_STEPS)]
        pr = coef_ref[sb, 2 * len(SCAN_STEPS)]
        pi = coef_ref[sb, 2 * len(SCAN_STEPS) + 1]
        cr, ci = carry_ref[sb, 0], carry_ref[sb, 1]
        for i in range(tt // SUBLANES):
            rows = slice(i * SUBLANES, (i + 1) * SUBLANES)
            xr = st_ref[sb, rows, 0:s]
            xi = st_ref[sb, rows, s:2 * s]
            for dlt, ar, ai in steps:
                sr = pltpu.roll(xr, dlt, 0)
                si = pltpu.roll(xi, dlt, 0)
                xr, xi = xr + (ar * sr - ai * si), xi + (ar * si + ai * sr)
            xr, xi = xr + (pr * cr - pi * ci), xi + (pr * ci + pi * cr)
            st_ref[sb, rows, 0:s] = xr
            st_ref[sb, rows, s:2 * s] = xi
            last = SUBLANES - 1
            cr = jnp.broadcast_to(xr[last:last + 1], xr.shape)
            ci = jnp.broadcast_to(xi[last:last + 1], xi.shape)
        carry_ref[sb, 0] = cr
        carry_ref[sb, 1] = ci
        y = jnp.dot(st_ref[sb].astype(BF16), cp_ref[sb], preferred_element_type=F32)
        y = y + d_ref[:, ch] * u.astype(F32)
        o_ref[:, ch] = jax.nn.gelu(y).astype(o_ref.dtype)


def _s5_pack(log_step, a_re, a_im, b_re, b_im, c_re, c_im):
    g, p = a_re.shape
    h = b_re.shape[-1]
    nblk = g // GROUPS_PER_BLOCK
    step = jnp.exp(log_step.astype(F32))[:, None]
    lam_re, lam_im = a_re.astype(F32), a_im.astype(F32)

    def power(k):
        mag = jnp.exp(k * lam_re * step)
        return mag * jnp.cos(k * lam_im * step), mag * jnp.sin(k * lam_im * step)

    abar_re, abar_im = power(1)
    den = lam_re * lam_re + lam_im * lam_im
    nr = abar_re - 1.0
    fr = (nr * lam_re + abar_im * lam_im) / den
    fi = (abar_im * lam_re - nr * lam_im) / den
    br, bi = b_re.astype(F32), b_im.astype(F32)
    bbar_re = fr[..., None] * br - fi[..., None] * bi
    bbar_im = fr[..., None] * bi + fi[..., None] * br
    eye = jnp.eye(GROUPS_PER_BLOCK, dtype=F32)

    def pack_b(bb):
        t = bb.reshape(nblk, GROUPS_PER_BLOCK, p, h)
        t = jnp.einsum("jgph,gk->jghkp", t, eye)
        return t.reshape(nblk, LANES, STATE_LANES)

    def pack_c(cc):
        t = cc.astype(F32).reshape(nblk, GROUPS_PER_BLOCK, h, p)
        t = jnp.einsum("jghp,gk->jgpkh", t, eye)
        return t.reshape(nblk, STATE_LANES, LANES)

    bpack = jnp.concatenate([pack_b(bbar_re), pack_b(bbar_im)], axis=-1).astype(BF16)
    cpack = jnp.concatenate([pack_c(c_re), -pack_c(c_im)], axis=1).astype(BF16)

    rows = jnp.arange(SUBLANES)

    def lanes(v):
        return v.reshape(nblk, 1, STATE_LANES)

    coefs = []
    for dlt in SCAN_STEPS:
        keep = (rows >= dlt).astype(F32)[None, :, None]
        pre, pim = power(float(dlt))
        coefs += [lanes(pre) * keep, lanes(pim) * keep]
    pw = [power(float(r + 1)) for r in range(SUBLANES)]
    coefs.append(jnp.stack([lanes(q[0])[:, 0] for q in pw], axis=1))
    coefs.append(jnp.stack([lanes(q[1])[:, 0] for q in pw], axis=1))
    coef = jnp.stack(coefs, axis=1)
    return bpack, cpack, coef


def _ssm(u, bpack, cpack, coef, dvec, bsz, seq):
    n, d = u.shape
    nblk = d // LANES
    tt = _tile(seq, SSM_CHUNK, BF16_ROWS)
    nt = seq // tt
    slabs = 2 if nblk % 2 == 0 else 1
    width = slabs * LANES
    row = lambda b, j, t: (b * nt + t, j)
    return pl.pallas_call(
        functools.partial(_ssm_kernel, slabs=slabs),
        out_shape=jax.ShapeDtypeStruct((n, d), BF16),
        grid=(bsz, nblk // slabs, nt),
        in_specs=[pl.BlockSpec((tt, width), row),
                  pl.BlockSpec((slabs, LANES, 2 * STATE_LANES), lambda b, j, t: (j, 0, 0)),
                  pl.BlockSpec((slabs, 2 * STATE_LANES, LANES), lambda b, j, t: (j, 0, 0)),
                  pl.BlockSpec((slabs, 2 * len(SCAN_STEPS) + 2, SUBLANES, STATE_LANES),
                               lambda b, j, t: (j, 0, 0, 0)),
                  pl.BlockSpec((1, width), lambda b, j, t: (0, j))],
        out_specs=pl.BlockSpec((tt, width), row),
        scratch_shapes=[pltpu.VMEM((slabs, tt, 2 * STATE_LANES), F32),
                        pltpu.VMEM((slabs, 2, SUBLANES, STATE_LANES), F32)],
        compiler_params=_params("arbitrary", "arbitrary", "arbitrary"),
        name="s5_ssm",
    )(u, bpack, cpack, coef, dvec.reshape(1, d))


def _ffn_up_kernel(h_ref, wu_ref, wv_ref, cwu_ref, cwv_ref, cbu_ref, cbv_ref, o_ref,
                   wub_ref, wvb_ref, carry_ref, *, tiles_per_batch, nsub):
    i = pl.program_id(1)

    @pl.when(i == 0)
    def _():
        wub_ref[...] = wu_ref[...].astype(BF16)
        wvb_ref[...] = wv_ref[...].astype(BF16)

    tm = h_ref.shape[0]
    ts = tm // nsub
    first = (i % tiles_per_batch) == 0
    prevs = []
    for slot in range(2):
        pv = carry_ref[slot]
        prevs.append(jnp.where(first, jnp.zeros_like(pv), pv))
    streams = ((wub_ref, cwu_ref, cbu_ref), (wvb_ref, cwv_ref, cbv_ref))
    for sb in range(nsub):
        rows = slice(sb * ts, (sb + 1) * ts)
        h = h_ref[rows, :]
        conv = []
        for slot, (wb_ref, cw_ref, cb_ref) in enumerate(streams):
            a = jnp.dot(h, wb_ref[...], preferred_element_type=F32)
            ext = jnp.concatenate([prevs[slot], a], axis=0)
            prevs[slot] = a[ts - SUBLANES:]
            cw = cw_ref[...]
            out = cw[CONV_WIDTH - 1:CONV_WIDTH] * a
            for lag in range(1, CONV_WIDTH):
                tap = CONV_WIDTH - 1 - lag
                out = out + cw[tap:tap + 1] * pltpu.roll(ext, lag, 0)[SUBLANES:]
            conv.append(out + cb_ref[...])
        cu, cv = conv
        o_ref[rows, :] = (cu * jax.nn.sigmoid(cu) * cv).astype(o_ref.dtype)
    for slot in range(2):
        carry_ref[slot] = prevs[slot]


def _ffn_up(h, w_up, conv_w, conv_b, layer, seq):
    n, d = h.shape
    ff = w_up.shape[2] // 2
    tm = _tile(seq, 1024, BF16_ROWS)
    tf = _tile(ff, 512, LANES)
    nf = ff // tf
    nsub = FFN_ROW_SPLITS if tm % (FFN_ROW_SPLITS * BF16_ROWS) == 0 else 1
    cb = conv_b.astype(F32).reshape(conv_b.shape[0], 1, 2 * ff)
    return pl.pallas_call(
        functools.partial(_ffn_up_kernel, tiles_per_batch=seq // tm, nsub=nsub),
        out_shape=jax.ShapeDtypeStruct((n, ff), BF16),
        grid=(nf, n // tm),
        in_specs=[pl.BlockSpec((tm, d), lambda f, i: (i, 0)),
                  pl.BlockSpec((None, d, tf), lambda f, i: (layer, 0, f)),
                  pl.BlockSpec((None, d, tf), lambda f, i: (layer, 0, nf + f)),
                  pl.BlockSpec((None, CONV_WIDTH, tf), lambda f, i: (layer, 0, f)),
                  pl.BlockSpec((None, CONV_WIDTH, tf), lambda f, i: (layer, 0, nf + f)),
                  pl.BlockSpec((None, 1, tf), lambda f, i: (layer, 0, f)),
                  pl.BlockSpec((None, 1, tf), lambda f, i: (layer, 0, nf + f))],
        out_specs=pl.BlockSpec((tm, tf), lambda f, i: (i, f)),
        scratch_shapes=[pltpu.VMEM((d, tf), BF16), pltpu.VMEM((d, tf), BF16),
                        pltpu.VMEM((2, SUBLANES, tf), F32)],
        compiler_params=_params("arbitrary", "arbitrary"),
        name="ffn_up_conv",
    )(h, w_up, w_up, conv_w.astype(F32), conv_w.astype(F32), cb, cb)


def _ffn_down_kernel(a_ref, w_ref, x_ref, gate_ref, *rest, kinds, emit_x, with_flog):
    na = _norm_arg_count(kinds)
    norm_refs = rest[:na]
    rest = rest[na:]
    if with_flog:
        wf_ref, rest = rest[0], rest[1:]
    outs = rest
    acc_ref = outs[0]
    k = pl.program_id(1)
    contrib = jnp.dot(a_ref[...], w_ref[...], preferred_element_type=F32)

    @pl.when(k == 0)
    def _():
        acc_ref[...] = contrib

    @pl.when(k > 0)
    def _():
        acc_ref[...] += contrib

    @pl.when(k == pl.num_programs(1) - 1)
    def _():
        xn = x_ref[...] + gate_ref[0] * acc_ref[...]
        q = 0
        if emit_x:
            outs[0][...] = xn
            q = 1
        first = _emit_norms(xn, kinds, norm_refs, outs[q:q + len(kinds)])
        if with_flog:
            hi = first.astype(BF16)
            lo = (first - hi.astype(F32)).astype(BF16)
            wf = wf_ref[...]
            whi = wf.astype(BF16)
            wlo = (wf - whi.astype(F32)).astype(BF16)
            nt = (((1,), (1,)), ((), ()))
            nh = wf.shape[0]
            both = lax.dot_general(jnp.concatenate([whi, wlo], axis=0), hi, nt,
                                   preferred_element_type=F32)
            outs[-1][...] = (both[:nh] + both[nh:]
                             + lax.dot_general(whi, lo, nt, preferred_element_type=F32))


def _ffn_down(act, x2, w_down, gate, kinds, norms, seq, *, emit_x, wf_t=None):
    n, ff = act.shape
    d = w_down.shape[1]
    tm = _tile(seq, 512, BF16_ROWS)
    tk = _tile(ff, 1536, LANES)
    tpb = seq // tm
    row = lambda i, k: (i, 0)
    bidx = lambda i, k: (i // tpb, 0, 0)
    nspecs, nargs = _norm_specs_args(kinds, norms, d, bidx)
    oshapes, ospecs = _norm_outs(kinds, n, d, tm, row)
    if emit_x:
        oshapes = [jax.ShapeDtypeStruct((n, d), F32)] + oshapes
        ospecs = [pl.BlockSpec((tm, d), row)] + ospecs
    assert oshapes[0].dtype == F32
    extra_specs, extra_args = [], []
    if wf_t is not None:
        nh = wf_t.shape[0]
        extra_specs = [pl.BlockSpec((nh, d), lambda i, k: (0, 0))]
        extra_args = [wf_t]
        oshapes = oshapes + [jax.ShapeDtypeStruct((nh, n), F32)]
        ospecs = ospecs + [pl.BlockSpec((nh, tm), lambda i, k: (0, i))]
    return pl.pallas_call(
        functools.partial(_ffn_down_kernel, kinds=tuple(kinds), emit_x=emit_x,
                          with_flog=wf_t is not None),
        out_shape=oshapes,
        grid=(n // tm, ff // tk),
        in_specs=[pl.BlockSpec((tm, tk), lambda i, k: (i, k)),
                  pl.BlockSpec((tk, d), lambda i, k: (k, 0)),
                  pl.BlockSpec((tm, d), row),
                  pl.BlockSpec((1, 1, d), bidx)] + nspecs + extra_specs,
        out_specs=ospecs,
        compiler_params=_params("arbitrary", "arbitrary"),
        name="ffn_down_res",
    )(act, w_down, x2, gate, *nargs, *extra_args)


def _fcum_kernel(z_ref, b_ref, tri_ref, o_ref):
    z = z_ref[...] + b_ref[...]
    lf = jnp.minimum(z, 0.0) - jnp.log1p(jnp.exp(-jnp.abs(z)))
    carry = jnp.zeros((z.shape[0], 1), F32)
    for cidx in range(z.shape[1] // LANES):
        sl = slice(cidx * LANES, (cidx + 1) * LANES)
        cs = jnp.dot(lf[:, sl], tri_ref[...], precision=lax.Precision.HIGHEST,
                     preferred_element_type=F32) + carry
        o_ref[:, sl] = cs * LOG2E
        carry = cs[:, LANES - 1:LANES]


def _fcum(flog_t, forget_b, bsz, seq):
    nh, n = flog_t.shape
    tri = (jnp.arange(LANES)[:, None] <= jnp.arange(LANES)[None, :]).astype(F32)
    return pl.pallas_call(
        _fcum_kernel,
        out_shape=jax.ShapeDtypeStruct((nh, n), F32),
        grid=(bsz,),
        in_specs=[pl.BlockSpec((nh, seq), lambda b: (0, b)),
                  pl.BlockSpec((nh, 1), lambda b: (0, 0)),
                  pl.BlockSpec((LANES, LANES), lambda b: (0, 0))],
        out_specs=pl.BlockSpec((nh, seq), lambda b: (0, b)),
        compiler_params=_params("arbitrary"),
        name="forget_cumsum",
    )(flog_t, forget_b.astype(F32).reshape(nh, 1), tri)


def _attn_kernel(q_ref, k_ref, v_ref, f_ref, mask_ref, o_ref, m_ref, l_ref, acc_ref, *, tk, heads):
    qi = pl.program_id(2)
    m_ref[...] = jnp.full_like(m_ref, -jnp.inf)
    l_ref[...] = jnp.zeros_like(l_ref)
    acc_ref[...] = jnp.zeros_like(acc_ref)
    ncol = tk // LANES

    def head_step(hd, cidx, masked):
        hs = slice(hd * HEAD_DIM, (hd + 1) * HEAD_DIM)
        k0 = pl.multiple_of(cidx * tk, tk)
        kc = k_ref[pl.ds(k0, tk), hs]
        vc = v_ref[pl.ds(k0, tk), hs]
        s = lax.dot_general(q_ref[:, hs], kc, (((1,), (1,)), ((), ())), preferred_element_type=F32)
        s = s - f_ref[hd, :, pl.ds(k0, tk)]
        if masked:
            s = s + mask_ref[...]
        cols_s = [s[:, j * LANES:(j + 1) * LANES] for j in range(ncol)]
        mc = cols_s[0]
        for sj in cols_s[1:]:
            mc = jnp.maximum(mc, sj)
        m_old = m_ref[hd]
        m_new = jnp.maximum(m_old, jnp.max(mc, axis=-1, keepdims=True))
        alpha = jnp.exp2(m_old - m_new)
        ps = [jnp.exp2(sj - m_new) for sj in cols_s]
        lsum = ps[0]
        for pj in ps[1:]:
            lsum = lsum + pj
        l_ref[hd] = alpha * l_ref[hd] + lsum
        p = jnp.concatenate(ps, axis=-1).astype(BF16)
        acc_ref[hd] = alpha * acc_ref[hd] + jnp.dot(p, vc, preferred_element_type=F32)
        m_ref[hd] = m_new

    def step(cidx, masked):
        for hd in range(heads):
            head_step(hd, cidx, masked)

    n_full = qi

    def pair_body(pidx, carry):
        step(2 * pidx, False)
        step(2 * pidx + 1, False)
        return carry

    lax.fori_loop(0, n_full // 2, pair_body, 0)

    @pl.when(n_full % 2 == 1)
    def _():
        step(n_full - 1, False)

    step(n_full, True)
    for hd in range(heads):
        l_row = jnp.sum(l_ref[hd], axis=-1, keepdims=True)
        o_ref[:, hd * HEAD_DIM:(hd + 1) * HEAD_DIM] = (acc_ref[hd] / l_row).astype(o_ref.dtype)


def _attention(q, kv, f_cum, bsz, seq):
    n, d = q.shape
    nh = d // HEAD_DIM
    heads = 2 if nh % 2 == 0 else 1
    tq = _tile(seq, 512, LANES)
    tk = tq
    nq = seq // tq
    width = heads * HEAD_DIM
    ngrp = nh // heads
    causal = jnp.where(jnp.arange(tk)[None, :] <= jnp.arange(tq)[:, None], 0.0, -jnp.inf).astype(F32)
    return pl.pallas_call(
        functools.partial(_attn_kernel, tk=tk, heads=heads),
        out_shape=jax.ShapeDtypeStruct((n, d), BF16),
        grid=(bsz, ngrp, nq),
        in_specs=[pl.BlockSpec((tq, width), lambda b, h, i: (b * nq + i, h)),
                  pl.BlockSpec((seq, width), lambda b, h, i: (b, h)),
                  pl.BlockSpec((seq, width), lambda b, h, i: (b, ngrp + h)),
                  pl.BlockSpec((heads, 1, seq), lambda b, h, i: (h, 0, b)),
                  pl.BlockSpec((tq, tk), lambda b, h, i: (0, 0))],
        out_specs=pl.BlockSpec((tq, width), lambda b, h, i: (b * nq + i, h)),
        scratch_shapes=[pltpu.VMEM((heads, tq, LANES), F32), pltpu.VMEM((heads, tq, LANES), F32),
                        pltpu.VMEM((heads, tq, HEAD_DIM), F32)],
        compiler_params=_params("arbitrary", "arbitrary", "arbitrary"),
        name="fox_attention",
    )(q, kv, kv, f_cum.reshape(nh, 1, n), causal)


def kernel(x, c, mod_w, mod_b, norm_mix_g, norm_ffn_g, ssm_w_in, ssm_log_step, ssm_a_re, ssm_a_im,
           ssm_b_re, ssm_b_im, ssm_c_re, ssm_c_im, ssm_d, ssm_glu_w, ssm_glu_b, ssm_w_out,
           kv_mod_w, kv_mod_b, kv_norm_g, kv_w, forget_b, attn_w_q, attn_w_out,
           ffn_w_up, ffn_conv_w, ffn_conv_b, ffn_w_down, final_norm_g):
    bsz, seq, d = x.shape
    n = bsz * seq
    depth = mod_w.shape[0]
    n_a = ssm_w_in.shape[0]
    assert n_a >= 1 and d % LANES == 0
    x2 = x.astype(F32).reshape(n, d)

    c_rep = jnp.broadcast_to(c.astype(F32)[:, :, None], (bsz, d, LANES))

    def vecs(m, count):
        return [m[:, k * d:(k + 1) * d].reshape(bsz, 1, d) for k in range(count)]

    mods = [vecs(_gemv(c_rep, mod_w, i, mod_b[i]), N_MOD) for i in range(depth)]
    sh_kv, sc_kv = vecs(_gemv(c_rep, kv_mod_w.reshape(1, *kv_mod_w.shape), 0, kv_mod_b), 2)

    def mix_norm(i):
        return (norm_mix_g[i], mods[i][0], mods[i][1])

    def after_ffn(i):
        if i + 1 == depth:
            return ["plain"], [(final_norm_g,)]
        kinds, norms = [], []
        if i + 1 == n_a:
            kinds.append("mod")
            norms.append((kv_norm_g, sh_kv, sc_kv))
        kinds.append("mod")
        norms.append(mix_norm(i + 1))
        return kinds, norms

    h = _norm_mod(x2, *mix_norm(0), seq)
    kv = f_cum = None
    hk = flog_t = None
    out = None
    for i in range(depth):
        sh_m, sc_m, g_m, sh_f, sc_f, g_f = mods[i]
        ffn_norm = (norm_ffn_g[i], sh_f, sc_f)
        if i == n_a:
            kv = _mm(hk, kv_w.reshape(1, *kv_w.shape), 0, 2 * d, name="kv_proj")
            f_cum = _fcum(flog_t, forget_b, bsz, seq)
        if i < n_a:
            u = _mm(h, ssm_w_in, i, d, name="ssm_in_proj")
            bpack, cpack, coef = _s5_pack(ssm_log_step[i], ssm_a_re[i], ssm_a_im[i], ssm_b_re[i],
                                          ssm_b_im[i], ssm_c_re[i], ssm_c_im[i])
            g = _ssm(u, bpack, cpack, coef, ssm_d[i].astype(F32), bsz, seq)
            z = _mm(g, ssm_glu_w, i, d, glu_bias=ssm_glu_b[i].astype(F32), name="ssm_glu")
            x2, h = _mm_res(z, ssm_w_out, i, x2, g_m, ["mod"], [ffn_norm], seq, name="ssm_out_proj")
        else:
            j = i - n_a
            q = _mm(h, attn_w_q, j, d, out_scale=HEAD_DIM ** -0.5 * LOG2E, name="q_proj")
            o = _attention(q, kv, f_cum, bsz, seq)
            x2, h = _mm_res(o, attn_w_out, j, x2, g_m, ["mod"], [ffn_norm], seq, name="attn_out_proj")
        kinds, norms = after_ffn(i)
        last = i + 1 == depth
        want_kv = (i + 1 == n_a) and not last
        wf_t = kv_w[:, 2 * d:].astype(F32).T if want_kv else None
        act = _ffn_up(h, ffn_w_up, ffn_conv_w, ffn_conv_b, i, seq)
        res = _ffn_down(act, x2, ffn_w_down[i].astype(BF16), g_f, kinds, norms, seq,
                        emit_x=not last, wf_t=wf_t)
        if last:
            out = res[0]
        else:
            x2 = res[0]
            if want_kv:
                hk, h, flog_t = res[1], res[2], res[3]
            else:
                h = res[1]
    return out.reshape(bsz, seq, d).astype(x.dtype)
```

```python
import functools
import math

import jax
import jax.numpy as jnp
from jax import lax
from jax.experimental import pallas as pl
from jax.experimental.pallas import tpu as pltpu

F32 = jnp.float32
BF16 = jnp.bfloat16

LANES = 128
SUBLANES = 8
BF16_ROWS = 16
VMEM_LIMIT_BYTES = 56 * 1024 * 1024

SSM_GROUP = 16
SSM_STATE = 64
HEAD_DIM = 128
CONV_WIDTH = 3
NORM_EPS = 1e-6
LOG2E = math.log2(math.e)
N_MOD = 6

GROUPS_PER_BLOCK = LANES // SSM_GROUP
STATE_LANES = GROUPS_PER_BLOCK * SSM_STATE
SCAN_STEPS = (1, 2, 4)
SSM_CHUNK = 512
SSM_SLABS = 4

def _params(*sem):
    return pltpu.CompilerParams(dimension_semantics=sem, vmem_limit_bytes=VMEM_LIMIT_BYTES)


def _tile(n, pref, align):
    if n <= pref:
        return n
    t = (pref // align) * align
    while t >= align:
        if n % t == 0:
            return t
        t -= align
    raise ValueError(f"no tile for {n}")


def _rms_hat(x):
    ms = jnp.mean(x * x, axis=-1, keepdims=True)
    return x * lax.rsqrt(ms + NORM_EPS)


def _modulate(y, shift, scale):
    return y * (1 + scale) + shift


def _gemv_kernel(c_ref, w_ref, b_ref, o_ref, cs_ref):
    @pl.when(pl.program_id(0) == 0)
    def _():
        cv = c_ref[...]
        cs_ref[...] = cv * jax.nn.sigmoid(cv)

    tn = w_ref.shape[1]
    for b in range(c_ref.shape[0]):
        cb = cs_ref[b]
        for lt in range(tn // LANES):
            sl = slice(lt * LANES, (lt + 1) * LANES)
            o_ref[b:b + 1, sl] = jnp.sum(w_ref[:, sl] * cb, axis=0, keepdims=True) + b_ref[:, sl]


def _gemv(c_rep, w_stack, layer, b):
    bsz, k, _ = c_rep.shape
    n = w_stack.shape[2]
    tn = _tile(n, 1024, LANES)
    return pl.pallas_call(
        _gemv_kernel,
        out_shape=jax.ShapeDtypeStruct((bsz, n), F32),
        grid=(n // tn,),
        in_specs=[pl.BlockSpec((bsz, k, LANES), lambda j: (0, 0, 0)),
                  pl.BlockSpec((None, k, tn), lambda j: (layer, 0, j)),
                  pl.BlockSpec((1, tn), lambda j: (0, j))],
        out_specs=pl.BlockSpec((bsz, tn), lambda j: (0, j)),
        scratch_shapes=[pltpu.VMEM((bsz, k, LANES), F32)],
        compiler_params=_params("arbitrary"),
        name="adaln_gemv",
    )(c_rep, w_stack, b.reshape(1, n))


def _norm_mod_kernel(x_ref, g_ref, sh_ref, sc_ref, o_ref):
    y = _rms_hat(x_ref[...]) * g_ref[...]
    o_ref[...] = _modulate(y, sh_ref[0], sc_ref[0]).astype(o_ref.dtype)


def _norm_mod(x2, g, shift, scale, seq):
    n, d = x2.shape
    tm = _tile(seq, 512, BF16_ROWS)
    tpb = seq // tm
    vec = pl.BlockSpec((1, 1, d), lambda i: (i // tpb, 0, 0))
    return pl.pallas_call(
        _norm_mod_kernel,
        out_shape=jax.ShapeDtypeStruct((n, d), BF16),
        grid=(n // tm,),
        in_specs=[pl.BlockSpec((tm, d), lambda i: (i, 0)),
                  pl.BlockSpec((1, d), lambda i: (0, 0)), vec, vec],
        out_specs=pl.BlockSpec((tm, d), lambda i: (i, 0)),
        compiler_params=_params("arbitrary"),
        name="norm_mod",
    )(x2, g.reshape(1, d), shift, scale)


def _mm_kernel(a_ref, w_ref, *rest, glu, out_scale):
    *rest, wb_ref = rest

    @pl.when(pl.program_id(1) == 0)
    def _():
        wb_ref[...] = w_ref[...].astype(BF16)

    acc = jnp.dot(a_ref[...], wb_ref[...], preferred_element_type=F32)
    if glu:
        g_ref, b_ref, o_ref = rest
        o_ref[...] = (g_ref[...].astype(F32) * jax.nn.sigmoid(acc + b_ref[...])).astype(o_ref.dtype)
    else:
        (o_ref,) = rest
        if out_scale != 1.0:
            acc = acc * out_scale
        o_ref[...] = acc.astype(o_ref.dtype)


def _mm(a, w_stack, layer, nout, *, glu_bias=None, out_scale=1.0, name):
    n, k = a.shape
    tm = _tile(n, 1024, BF16_ROWS)
    tn = _tile(nout, 1024, LANES)
    if w_stack.ndim == 2:
        w_spec = pl.BlockSpec((k, tn), lambda j, i: (0, j))
    else:
        w_spec = pl.BlockSpec((None, k, tn), lambda j, i: (layer, 0, j))
    in_specs = [pl.BlockSpec((tm, k), lambda j, i: (i, 0)), w_spec]
    args = [a, w_stack]
    if glu_bias is not None:
        in_specs += [pl.BlockSpec((tm, tn), lambda j, i: (i, j)),
                     pl.BlockSpec((1, tn), lambda j, i: (0, j))]
        args += [a, glu_bias.reshape(1, nout)]
    return pl.pallas_call(
        functools.partial(_mm_kernel, glu=glu_bias is not None, out_scale=out_scale),
        out_shape=jax.ShapeDtypeStruct((n, nout), BF16),
        grid=(nout // tn, n // tm),
        in_specs=in_specs,
        out_specs=pl.BlockSpec((tm, tn), lambda j, i: (i, j)),
        scratch_shapes=[pltpu.VMEM((k, tn), BF16)],
        compiler_params=_params("arbitrary", "arbitrary"),
        name=name,
    )(*args)


def _norm_arg_count(kinds):
    return sum(3 if kd == "mod" else 1 for kd in kinds)


def _emit_norms(xn, kinds, norm_refs, out_refs):
    xhat = _rms_hat(xn)
    first = None
    p = 0
    for kd, o_ref in zip(kinds, out_refs):
        if kd == "mod":
            g_ref, sh_ref, sc_ref = norm_refs[p:p + 3]
            p += 3
            val = _modulate(xhat * g_ref[...], sh_ref[0], sc_ref[0])
        else:
            val = xhat * norm_refs[p][...]
            p += 1
        if first is None:
            first = val
        o_ref[...] = val.astype(o_ref.dtype)
    return first


def _norm_specs_args(kinds, norms, d, bidx):
    specs, args = [], []
    for kd, nm in zip(kinds, norms):
        specs.append(pl.BlockSpec((1, d), lambda *ix: (0, 0)))
        args.append(nm[0].reshape(1, d))
        if kd == "mod":
            for v in nm[1:]:
                specs.append(pl.BlockSpec((1, 1, d), bidx))
                args.append(v)
    return specs, args


def _norm_outs(kinds, n, d, tm, row_map):
    shapes = [jax.ShapeDtypeStruct((n, d), BF16 if kd == "mod" else F32) for kd in kinds]
    specs = [pl.BlockSpec((tm, d), row_map) for _ in kinds]
    return shapes, specs


def _mm_res_kernel(a_ref, w_ref, x_ref, gate_ref, *rest, kinds):
    *rest, wb_ref = rest
    na = _norm_arg_count(kinds)
    norm_refs, outs = rest[:na], rest[na:]

    @pl.when(pl.program_id(0) == 0)
    def _():
        wb_ref[...] = w_ref[...].astype(BF16)

    acc = jnp.dot(a_ref[...], wb_ref[...], preferred_element_type=F32)
    xn = x_ref[...] + gate_ref[0] * acc
    outs[0][...] = xn
    _emit_norms(xn, kinds, norm_refs, outs[1:])


def _mm_res(a, w_stack, layer, x2, gate, kinds, norms, seq, *, name):
    n, k = a.shape
    d = w_stack.shape[2]
    tm = _tile(seq, 512, BF16_ROWS)
    tpb = seq // tm
    bidx = lambda i: (i // tpb, 0, 0)
    row = lambda i: (i, 0)
    nspecs, nargs = _norm_specs_args(kinds, norms, d, bidx)
    oshapes, ospecs = _norm_outs(kinds, n, d, tm, row)
    return pl.pallas_call(
        functools.partial(_mm_res_kernel, kinds=tuple(kinds)),
        out_shape=[jax.ShapeDtypeStruct((n, d), F32)] + oshapes,
        grid=(n // tm,),
        in_specs=[pl.BlockSpec((tm, k), row),
                  pl.BlockSpec((None, k, d), lambda i: (layer, 0, 0), pipeline_mode=pl.Buffered(1)),
                  pl.BlockSpec((tm, d), row),
                  pl.BlockSpec((1, 1, d), bidx)] + nspecs,
        out_specs=[pl.BlockSpec((tm, d), row)] + ospecs,
        scratch_shapes=[pltpu.VMEM((k, d), BF16)],
        compiler_params=_params("arbitrary"),
        name=name,
    )(a, w_stack, x2, gate, *nargs)


def _ssm_kernel(u_ref, bp_ref, cp_ref, coef_ref, d_ref, o_ref, upad_ref, st_ref, ypad_ref,
                carry_ref, *, slabs, seg_len):
    s = STATE_LANES
    nseg = SUBLANES
    pitch = seg_len + 1
    nlt = s // LANES

    @pl.when(pl.program_id(2) == 0)
    def _():
        carry_ref[...] = jnp.zeros_like(carry_ref)

    def cmul(ar, ai, xr, xi):
        return ar * xr - ai * xi, ar * xi + ai * xr

    def load_rows(sb, half, rows):
        return jnp.concatenate([st_ref[sb, half * nlt + j, rows, :] for j in range(nlt)], axis=-1)

    def store_rows(sb, half, rows, val):
        for j in range(nlt):
            st_ref[sb, half * nlt + j, rows, :] = val[:, j * LANES:(j + 1) * LANES]

    for sb in range(slabs):
        ch = slice(sb * LANES, (sb + 1) * LANES)
        u32 = u_ref[:, ch].astype(F32)
        upad_ref[sb] = jnp.zeros(upad_ref.shape[1:], F32)
        for k in range(nseg):
            upad_ref[sb, pl.ds(k * pitch, seg_len), :] = u32[k * seg_len:(k + 1) * seg_len]
        bu = jnp.dot(upad_ref[sb].astype(BF16), bp_ref[sb], preferred_element_type=F32)
        for j in range(2 * nlt):
            st_ref[sb, j] = bu[:, j * LANES:(j + 1) * LANES]
        ar, ai = coef_ref[sb, 0], coef_ref[sb, 1]

        sr = jnp.zeros((nseg, s), F32)
        si = jnp.zeros((nseg, s), F32)
        for t in range(seg_len):
            rows = pl.ds(t, nseg, stride=pitch)
            pr_, pi_ = cmul(ar, ai, sr, si)
            sr = pr_ + load_rows(sb, 0, rows)
            si = pi_ + load_rows(sb, 1, rows)
            store_rows(sb, 0, rows, sr)
            store_rows(sb, 1, rows, si)

        tr, ti = sr, si
        for q, dlt in enumerate(SCAN_STEPS):
            hr, hi = coef_ref[sb, 2 + 2 * q], coef_ref[sb, 3 + 2 * q]
            dr, di = cmul(hr, hi, pltpu.roll(tr, dlt, 0), pltpu.roll(ti, dlt, 0))
            tr, ti = tr + dr, ti + di
        cr, ci = carry_ref[sb, 0], carry_ref[sb, 1]
        dr, di = cmul(coef_ref[sb, 8], coef_ref[sb, 9], cr, ci)
        tr, ti = tr + dr, ti + di
        row0 = lax.broadcasted_iota(jnp.int32, (nseg, s), 0) == 0
        zr = jnp.where(row0, cr, pltpu.roll(tr, 1, 0))
        zi = jnp.where(row0, ci, pltpu.roll(ti, 1, 0))
        carry_ref[sb, 0] = jnp.broadcast_to(tr[nseg - 1:nseg], tr.shape)
        carry_ref[sb, 1] = jnp.broadcast_to(ti[nseg - 1:nseg], ti.shape)

        for t in range(seg_len):
            rows = pl.ds(t, nseg, stride=pitch)
            zr, zi = cmul(ar, ai, zr, zi)
            store_rows(sb, 0, rows, load_rows(sb, 0, rows) + zr)
            store_rows(sb, 1, rows, load_rows(sb, 1, rows) + zi)

        states = jnp.concatenate([st_ref[sb, j] for j in range(2 * nlt)], axis=-1).astype(BF16)
        ypad_ref[sb] = jnp.dot(states, cp_ref[sb], preferred_element_type=F32)
        y = jnp.concatenate([ypad_ref[sb, pl.ds(k * pitch, seg_len), :] for k in range(nseg)], axis=0)
        y = y + d_ref[:, ch] * u32
        o_ref[:, ch] = jax.nn.gelu(y).astype(o_ref.dtype)


def _ssm_chunk(seq):
    tt = _tile(seq, SSM_CHUNK, SUBLANES * SUBLANES)
    return tt, tt // SUBLANES


def _s5_pack(log_step, a_re, a_im, b_re, b_im, c_re, c_im, seg_len):
    g, p = a_re.shape
    h = b_re.shape[-1]
    nblk = g // GROUPS_PER_BLOCK
    step = jnp.exp(log_step.astype(F32))[:, None]
    lam_re, lam_im = a_re.astype(F32), a_im.astype(F32)

    def power(k):
        mag = jnp.exp(k * lam_re * step)
        return mag * jnp.cos(k * lam_im * step), mag * jnp.sin(k * lam_im * step)

    abar_re, abar_im = power(1)
    den = lam_re * lam_re + lam_im * lam_im
    nr = abar_re - 1.0
    fr = (nr * lam_re + abar_im * lam_im) / den
    fi = (abar_im * lam_re - nr * lam_im) / den
    br, bi = b_re.astype(F32), b_im.astype(F32)
    bbar_re = fr[..., None] * br - fi[..., None] * bi
    bbar_im = fr[..., None] * bi + fi[..., None] * br
    eye = jnp.eye(GROUPS_PER_BLOCK, dtype=F32)

    def pack_b(bb):
        t = bb.reshape(nblk, GROUPS_PER_BLOCK, p, h)
        t = jnp.einsum("jgph,gk->jghkp", t, eye)
        return t.reshape(nblk, LANES, STATE_LANES)

    def pack_c(cc):
        t = cc.astype(F32).reshape(nblk, GROUPS_PER_BLOCK, h, p)
        t = jnp.einsum("jghp,gk->jgpkh", t, eye)
        return t.reshape(nblk, STATE_LANES, LANES)

    bpack = jnp.concatenate([pack_b(bbar_re), pack_b(bbar_im)], axis=-1).astype(BF16)
    cpack = jnp.concatenate([pack_c(c_re), -pack_c(c_im)], axis=1).astype(BF16)

    rows = jnp.arange(SUBLANES)

    def lanes(v):
        return v.reshape(nblk, 1, STATE_LANES)

    ones = jnp.ones((1, SUBLANES, 1), F32)
    coefs = [lanes(abar_re) * ones, lanes(abar_im) * ones]
    for dlt in SCAN_STEPS:
        keep = (rows >= dlt).astype(F32)[None, :, None]
        pre, pim = power(float(seg_len * dlt))
        coefs += [lanes(pre) * keep, lanes(pim) * keep]
    pw = [power(float(seg_len * (r + 1))) for r in range(SUBLANES)]
    coefs.append(jnp.stack([lanes(q[0])[:, 0] for q in pw], axis=1))
    coefs.append(jnp.stack([lanes(q[1])[:, 0] for q in pw], axis=1))
    coef = jnp.stack(coefs, axis=1)
    return bpack, cpack, coef


def _ssm(u, bpack, cpack, coef, dvec, bsz, seq):
    n, d = u.shape
    nblk = d // LANES
    tt, seg_len = _ssm_chunk(seq)
    nt = seq // tt
    slabs = next(c for c in (SSM_SLABS, 2, 1) if nblk % c == 0)
    width = slabs * LANES
    prow =-(-(SUBLANES * (seg_len + 1)) // BF16_ROWS) * BF16_ROWS
    row = lambda b, j, t: (b * nt + t, j)
    return pl.pallas_call(
        functools.partial(_ssm_kernel, slabs=slabs, seg_len=seg_len),
        out_shape=jax.ShapeDtypeStruct((n, d), BF16),
        grid=(bsz, nblk // slabs, nt),
        in_specs=[pl.BlockSpec((tt, width), row),
                  pl.BlockSpec((slabs, LANES, 2 * STATE_LANES), lambda b, j, t: (j, 0, 0)),
                  pl.BlockSpec((slabs, 2 * STATE_LANES, LANES), lambda b, j, t: (j, 0, 0)),
                  pl.BlockSpec((slabs, 2 * len(SCAN_STEPS) + 4, SUBLANES, STATE_LANES),
                               lambda b, j, t: (j, 0, 0, 0)),
                  pl.BlockSpec((1, width), lambda b, j, t: (0, j))],
        out_specs=pl.BlockSpec((tt, width), row),
        scratch_shapes=[pltpu.VMEM((slabs, prow, LANES), F32),
                        pltpu.VMEM((slabs, 2 * STATE_LANES // LANES, prow, LANES), F32),
                        pltpu.VMEM((slabs, prow, LANES), F32),
                        pltpu.VMEM((slabs, 2, SUBLANES, STATE_LANES), F32)],
        compiler_params=_params("arbitrary", "arbitrary", "arbitrary"),
        name="s5_ssm",
    )(u, bpack, cpack, coef, dvec.reshape(1, d))


def _ffn_up_kernel(h_ref, wu_ref, wv_ref, cwu_ref, cwv_ref, cbu_ref, cbv_ref, o_ref,
                   wub_ref, wvb_ref, carry_ref, *, tiles_per_batch):
    i = pl.program_id(1)
    tm = h_ref.shape[0]

    @pl.when(i == 0)
    def _():
        wub_ref[...] = wu_ref[...].astype(BF16)
        wvb_ref[...] = wv_ref[...].astype(BF16)

    first = (i % tiles_per_batch) == 0
    h = h_ref[...]
    conv = []
    for stream, (wb_ref, cw_ref, cb_ref) in enumerate(((wub_ref, cwu_ref, cbu_ref),
                                                        (wvb_ref, cwv_ref, cbv_ref))):
        a = jnp.dot(h, wb_ref[...], preferred_element_type=F32)
        pv = carry_ref[stream]
        prev = jnp.where(first, jnp.zeros_like(pv), pv)
        carry_ref[stream] = a[tm - SUBLANES:]
        ext = jnp.concatenate([prev, a], axis=0)
        cw = cw_ref[...]
        out = cw[CONV_WIDTH - 1:CONV_WIDTH] * a
        for lag in range(1, CONV_WIDTH):
            tap = CONV_WIDTH - 1 - lag
            out = out + cw[tap:tap + 1] * pltpu.roll(ext, lag, 0)[SUBLANES:]
        conv.append(out + cb_ref[...])
    cu, cv = conv
    o_ref[...] = (cu * jax.nn.sigmoid(cu) * cv).astype(o_ref.dtype)


def _ffn_up(h, w_up, conv_w, conv_b, layer, seq):
    n, d = h.shape
    ff = w_up.shape[2] // 2
    tm = _tile(seq, 1024, BF16_ROWS)
    tf = _tile(ff, 512, LANES)
    nf = ff // tf
    cb = conv_b.astype(F32).reshape(conv_b.shape[0], 1, 2 * ff)
    return pl.pallas_call(
        functools.partial(_ffn_up_kernel, tiles_per_batch=seq // tm),
        out_shape=jax.ShapeDtypeStruct((n, ff), BF16),
        grid=(nf, n // tm),
        in_specs=[pl.BlockSpec((tm, d), lambda f, i: (i, 0)),
                  pl.BlockSpec((None, d, tf), lambda f, i: (layer, 0, f)),
                  pl.BlockSpec((None, d, tf), lambda f, i: (layer, 0, nf + f)),
                  pl.BlockSpec((None, CONV_WIDTH, tf), lambda f, i: (layer, 0, f)),
                  pl.BlockSpec((None, CONV_WIDTH, tf), lambda f, i: (layer, 0, nf + f)),
                  pl.BlockSpec((None, 1, tf), lambda f, i: (layer, 0, f)),
                  pl.BlockSpec((None, 1, tf), lambda f, i: (layer, 0, nf + f))],
        out_specs=pl.BlockSpec((tm, tf), lambda f, i: (i, f)),
        scratch_shapes=[pltpu.VMEM((d, tf), BF16), pltpu.VMEM((d, tf), BF16),
                        pltpu.VMEM((2, SUBLANES, tf), F32)],
        compiler_params=_params("arbitrary", "arbitrary"),
        name="ffn_up_conv",
    )(h, w_up, w_up, conv_w.astype(F32), conv_w.astype(F32), cb, cb)


def _ffn_down_kernel(a_ref, w_ref, x_ref, gate_ref, *rest, kinds, emit_x, with_flog):
    na = _norm_arg_count(kinds)
    norm_refs = rest[:na]
    rest = rest[na:]
    if with_flog:
        wf_ref, rest = rest[0], rest[1:]
    outs = rest
    acc_ref = outs[0]
    k = pl.program_id(1)
    contrib = jnp.dot(a_ref[...], w_ref[...], preferred_element_type=F32)

    @pl.when(k == 0)
    def _():
        acc_ref[...] = contrib

    @pl.when(k > 0)
    def _():
        acc_ref[...] += contrib

    @pl.when(k == pl.num_programs(1) - 1)
    def _():
        xn = x_ref[...] + gate_ref[0] * acc_ref[...]
        q = 0
        if emit_x:
            outs[0][...] = xn
            q = 1
        first = _emit_norms(xn, kinds, norm_refs, outs[q:q + len(kinds)])
        if with_flog:
            hi = first.astype(BF16)
            lo = (first - hi.astype(F32)).astype(BF16)
            wf = wf_ref[...]
            whi = wf.astype(BF16)
            wlo = (wf - whi.astype(F32)).astype(BF16)
            nt = (((1,), (1,)), ((), ()))
            nh = wf.shape[0]
            both = lax.dot_general(jnp.concatenate([whi, wlo], axis=0), hi, nt,
                                   preferred_element_type=F32)
            outs[-1][...] = (both[:nh] + both[nh:]
                             + lax.dot_general(whi, lo, nt, preferred_element_type=F32))


def _ffn_down(act, x2, w_down, layer, gate, kinds, norms, seq, *, emit_x, wf_t=None):
    n, ff = act.shape
    d = w_down.shape[2]
    tm = _tile(seq, 512, BF16_ROWS)
    tk = _tile(ff, 1536, LANES)
    tpb = seq // tm
    row = lambda i, k: (i, 0)
    bidx = lambda i, k: (i // tpb, 0, 0)
    nspecs, nargs = _norm_specs_args(kinds, norms, d, bidx)
    oshapes, ospecs = _norm_outs(kinds, n, d, tm, row)
    if emit_x:
        oshapes = [jax.ShapeDtypeStruct((n, d), F32)] + oshapes
        ospecs = [pl.BlockSpec((tm, d), row)] + ospecs
    assert oshapes[0].dtype == F32
    extra_specs, extra_args = [], []
    if wf_t is not None:
        nh = wf_t.shape[0]
        extra_specs = [pl.BlockSpec((nh, d), lambda i, k: (0, 0))]
        extra_args = [wf_t]
        oshapes = oshapes + [jax.ShapeDtypeStruct((nh, n), F32)]
        ospecs = ospecs + [pl.BlockSpec((nh, tm), lambda i, k: (0, i))]
    return pl.pallas_call(
        functools.partial(_ffn_down_kernel, kinds=tuple(kinds), emit_x=emit_x,
                          with_flog=wf_t is not None),
        out_shape=oshapes,
        grid=(n // tm, ff // tk),
        in_specs=[pl.BlockSpec((tm, tk), lambda i, k: (i, k)),
                  pl.BlockSpec((None, tk, d), lambda i, k: (layer, k, 0)),
                  pl.BlockSpec((tm, d), row),
                  pl.BlockSpec((1, 1, d), bidx)] + nspecs + extra_specs,
        out_specs=ospecs,
        compiler_params=_params("arbitrary", "arbitrary"),
        name="ffn_down_res",
    )(act, w_down, x2, gate, *nargs, *extra_args)


def _fcum_kernel(z_ref, b_ref, tri_ref, o_ref):
    z = z_ref[...] + b_ref[...]
    lf = jnp.minimum(z, 0.0) - jnp.log1p(jnp.exp(-jnp.abs(z)))
    carry = jnp.zeros((z.shape[0], 1), F32)
    for cidx in range(z.shape[1] // LANES):
        sl = slice(cidx * LANES, (cidx + 1) * LANES)
        cs = jnp.dot(lf[:, sl], tri_ref[...], precision=lax.Precision.HIGHEST,
                     preferred_element_type=F32) + carry
        o_ref[:, sl] = cs * LOG2E
        carry = cs[:, LANES - 1:LANES]


def _fcum(flog_t, forget_b, bsz, seq):
    nh, n = flog_t.shape
    tri = (jnp.arange(LANES)[:, None] <= jnp.arange(LANES)[None, :]).astype(F32)
    return pl.pallas_call(
        _fcum_kernel,
        out_shape=jax.ShapeDtypeStruct((nh, n), F32),
        grid=(bsz,),
        in_specs=[pl.BlockSpec((nh, seq), lambda b: (0, b)),
                  pl.BlockSpec((nh, 1), lambda b: (0, 0)),
                  pl.BlockSpec((LANES, LANES), lambda b: (0, 0))],
        out_specs=pl.BlockSpec((nh, seq), lambda b: (0, b)),
        compiler_params=_params("arbitrary"),
        name="forget_cumsum",
    )(flog_t, forget_b.astype(F32).reshape(nh, 1), tri)


def _attn_kernel(q_ref, k_ref, v_ref, f_ref, mask_ref, o_ref, m_ref, l_ref, acc_ref, *, tk, heads):
    qi = pl.program_id(2)
    m_ref[...] = jnp.full_like(m_ref, -jnp.inf)
    l_ref[...] = jnp.zeros_like(l_ref)
    acc_ref[...] = jnp.zeros_like(acc_ref)
    ncol = tk // LANES

    def head_step(hd, cidx, masked):
        hs = slice(hd * HEAD_DIM, (hd + 1) * HEAD_DIM)
        k0 = pl.multiple_of(cidx * tk, tk)
        kc = k_ref[pl.ds(k0, tk), hs]
        vc = v_ref[pl.ds(k0, tk), hs]
        s = lax.dot_general(q_ref[:, hs], kc, (((1,), (1,)), ((), ())), preferred_element_type=F32)
        s = s - f_ref[hd, :, pl.ds(k0, tk)]
        if masked:
            s = s + mask_ref[...]
        cols_s = [s[:, j * LANES:(j + 1) * LANES] for j in range(ncol)]
        mc = cols_s[0]
        for sj in cols_s[1:]:
            mc = jnp.maximum(mc, sj)
        m_old = m_ref[hd]
        m_new = jnp.maximum(m_old, jnp.max(mc, axis=-1, keepdims=True))
        alpha = jnp.exp2(m_old - m_new)
        ps = [jnp.exp2(sj - m_new) for sj in cols_s]
        lsum = ps[0]
        for pj in ps[1:]:
            lsum = lsum + pj
        l_ref[hd] = alpha * l_ref[hd] + lsum
        p = jnp.concatenate(ps, axis=-1).astype(BF16)
        acc_ref[hd] = alpha * acc_ref[hd] + jnp.dot(p, vc, preferred_element_type=F32)
        m_ref[hd] = m_new

    def step(cidx, masked):
        for hd in range(heads):
            head_step(hd, cidx, masked)

    n_full = qi

    def pair_body(pidx, carry):
        step(2 * pidx, False)
        step(2 * pidx + 1, False)
        return carry

    lax.fori_loop(0, n_full // 2, pair_body, 0)

    @pl.when(n_full % 2 == 1)
    def _():
        step(n_full - 1, False)

    step(n_full, True)
    for hd in range(heads):
        l_row = jnp.sum(l_ref[hd], axis=-1, keepdims=True)
        o_ref[:, hd * HEAD_DIM:(hd + 1) * HEAD_DIM] = (acc_ref[hd] / l_row).astype(o_ref.dtype)


def _attention(q, kv, f_cum, bsz, seq):
    n, d = q.shape
    nh = d // HEAD_DIM
    heads = 2 if nh % 2 == 0 else 1
    tq = _tile(seq, 512, LANES)
    tk = tq
    nq = seq // tq
    width = heads * HEAD_DIM
    ngrp = nh // heads
    causal = jnp.where(jnp.arange(tk)[None, :] <= jnp.arange(tq)[:, None], 0.0, -jnp.inf).astype(F32)
    return pl.pallas_call(
        functools.partial(_attn_kernel, tk=tk, heads=heads),
        out_shape=jax.ShapeDtypeStruct((n, d), BF16),
        grid=(bsz, ngrp, nq),
        in_specs=[pl.BlockSpec((tq, width), lambda b, h, i: (b * nq + i, h)),
                  pl.BlockSpec((seq, width), lambda b, h, i: (b, h)),
                  pl.BlockSpec((seq, width), lambda b, h, i: (b, ngrp + h)),
                  pl.BlockSpec((heads, 1, seq), lambda b, h, i: (h, 0, b)),
                  pl.BlockSpec((tq, tk), lambda b, h, i: (0, 0))],
        out_specs=pl.BlockSpec((tq, width), lambda b, h, i: (b * nq + i, h)),
        scratch_shapes=[pltpu.VMEM((heads, tq, LANES), F32), pltpu.VMEM((heads, tq, LANES), F32),
                        pltpu.VMEM((heads, tq, HEAD_DIM), F32)],
        compiler_params=_params("arbitrary", "arbitrary", "arbitrary"),
        name="fox_attention",
    )(q, kv, kv, f_cum.reshape(nh, 1, n), causal)


def kernel(x, c, mod_w, mod_b, norm_mix_g, norm_ffn_g, ssm_w_in, ssm_log_step, ssm_a_re, ssm_a_im,
           ssm_b_re, ssm_b_im, ssm_c_re, ssm_c_im, ssm_d, ssm_glu_w, ssm_glu_b, ssm_w_out,
           kv_mod_w, kv_mod_b, kv_norm_g, kv_w, forget_b, attn_w_q, attn_w_out,
           ffn_w_up, ffn_conv_w, ffn_conv_b, ffn_w_down, final_norm_g):
    bsz, seq, d = x.shape
    n = bsz * seq
    depth = mod_w.shape[0]
    n_a = ssm_w_in.shape[0]
    assert n_a >= 1 and d % LANES == 0
    x2 = x.astype(F32).reshape(n, d)

    c_rep = jnp.broadcast_to(c.astype(F32)[:, :, None], (bsz, d, LANES))

    def vecs(m, count):
        return [m[:, k * d:(k + 1) * d].reshape(bsz, 1, d) for k in range(count)]

    mods = [vecs(_gemv(c_rep, mod_w, i, mod_b[i]), N_MOD) for i in range(depth)]
    sh_kv, sc_kv = vecs(_gemv(c_rep, kv_mod_w.reshape(1, *kv_mod_w.shape), 0, kv_mod_b), 2)

    def mix_norm(i):
        return (norm_mix_g[i], mods[i][0], mods[i][1])

    def after_ffn(i):
        if i + 1 == depth:
            return ["plain"], [(final_norm_g,)]
        kinds, norms = [], []
        if i + 1 == n_a:
            kinds.append("mod")
            norms.append((kv_norm_g, sh_kv, sc_kv))
        kinds.append("mod")
        norms.append(mix_norm(i + 1))
        return kinds, norms

    h = _norm_mod(x2, *mix_norm(0), seq)
    w_down_bf = ffn_w_down.astype(BF16)
    kv = f_cum = None
    hk = flog_t = None
    out = None
    for i in range(depth):
        sh_m, sc_m, g_m, sh_f, sc_f, g_f = mods[i]
        ffn_norm = (norm_ffn_g[i], sh_f, sc_f)
        if i == n_a:
            kv = _mm(hk, kv_w, 0, 2 * d, name="kv_proj")
            f_cum = _fcum(flog_t, forget_b, bsz, seq)
        if i < n_a:
            u = _mm(h, ssm_w_in, i, d, name="ssm_in_proj")
            bpack, cpack, coef = _s5_pack(ssm_log_step[i], ssm_a_re[i], ssm_a_im[i], ssm_b_re[i],
                                          ssm_b_im[i], ssm_c_re[i], ssm_c_im[i], _ssm_chunk(seq)[1])
            g = _ssm(u, bpack, cpack, coef, ssm_d[i].astype(F32), bsz, seq)
            z = _mm(g, ssm_glu_w, i, d, glu_bias=ssm_glu_b[i].astype(F32), name="ssm_glu")
            x2, h = _mm_res(z, ssm_w_out, i, x2, g_m, ["mod"], [ffn_norm], seq, name="ssm_out_proj")
        else:
            j = i - n_a
            q = _mm(h, attn_w_q, j, d, out_scale=HEAD_DIM ** -0.5 * LOG2E, name="q_proj")
            o = _attention(q, kv, f_cum, bsz, seq)
            x2, h = _mm_res(o, attn_w_out, j, x2, g_m, ["mod"], [ffn_norm], seq, name="attn_out_proj")
        kinds, norms = after_ffn(i)
        last = i + 1 == depth
        want_kv = (i + 1 == n_a) and not last
        wf_t = kv_w[:, 2 * d:].astype(F32).T if want_kv else None
        act = _ffn_up(h, ffn_w_up, ffn_conv_w, ffn_conv_b, i, seq)
        res = _ffn_down(act, x2, w_down_bf, i, g_f, kinds, norms, seq, emit_x=not last, wf_t=wf_t)
        if last:
            out = res[0]
        else:
            x2 = res[0]
            if want_kv:
                hk, h, flog_t = res[1], res[2], res[3]
            else:
                h = res[1]
    return out.reshape(bsz, seq, d).astype(x.dtype)
```

```python
import functools
import math

import jax
import jax.numpy as jnp
from jax import lax
from jax.experimental import pallas as pl
from jax.experimental.pallas import tpu as pltpu

F32 = jnp.float32
BF16 = jnp.bfloat16

LANES = 128
SUBLANES = 8
BF16_ROWS = 16
VMEM_LIMIT_BYTES = 56 * 1024 * 1024

SSM_GROUP = 16
SSM_STATE = 64
HEAD_DIM = 128
CONV_WIDTH = 3
NORM_EPS = 1e-6
LOG2E = math.log2(math.e)
N_MOD = 6

GROUPS_PER_BLOCK = LANES // SSM_GROUP
STATE_LANES = GROUPS_PER_BLOCK * SSM_STATE
SCAN_STEPS = (1, 2, 4)
SSM_CHUNK = 512
SSM_SLABS = 4
ATTN_HEADS = 4

def _params(*sem):
    return pltpu.CompilerParams(dimension_semantics=sem, vmem_limit_bytes=VMEM_LIMIT_BYTES)


def _tile(n, pref, align):
    if n <= pref:
        return n
    t = (pref // align) * align
    while t >= align:
        if n % t == 0:
            return t
        t -= align
    raise ValueError(f"no tile for {n}")


def _rms_hat(x):
    ms = jnp.mean(x * x, axis=-1, keepdims=True)
    return x * lax.rsqrt(ms + NORM_EPS)


def _modulate(y, shift, scale):
    return y * (1 + scale) + shift


def _gemv_kernel(c_ref, w_ref, b_ref, o_ref, cs_ref):
    @pl.when(pl.program_id(0) == 0)
    def _():
        cv = c_ref[...]
        cs_ref[...] = cv * jax.nn.sigmoid(cv)

    tn = w_ref.shape[1]
    for b in range(c_ref.shape[0]):
        cb = cs_ref[b]
        for lt in range(tn // LANES):
            sl = slice(lt * LANES, (lt + 1) * LANES)
            o_ref[b:b + 1, sl] = jnp.sum(w_ref[:, sl] * cb, axis=0, keepdims=True) + b_ref[:, sl]


def _gemv(c_rep, w_stack, layer, b):
    bsz, k, _ = c_rep.shape
    n = w_stack.shape[2]
    tn = _tile(n, 1024, LANES)
    return pl.pallas_call(
        _gemv_kernel,
        out_shape=jax.ShapeDtypeStruct((bsz, n), F32),
        grid=(n // tn,),
        in_specs=[pl.BlockSpec((bsz, k, LANES), lambda j: (0, 0, 0)),
                  pl.BlockSpec((None, k, tn), lambda j: (layer, 0, j)),
                  pl.BlockSpec((1, tn), lambda j: (0, j))],
        out_specs=pl.BlockSpec((bsz, tn), lambda j: (0, j)),
        scratch_shapes=[pltpu.VMEM((bsz, k, LANES), F32)],
        compiler_params=_params("arbitrary"),
        name="adaln_gemv",
    )(c_rep, w_stack, b.reshape(1, n))


def _norm_mod_kernel(x_ref, g_ref, sh_ref, sc_ref, o_ref):
    y = _rms_hat(x_ref[...]) * g_ref[...]
    o_ref[...] = _modulate(y, sh_ref[0], sc_ref[0]).astype(o_ref.dtype)


def _norm_mod(x2, g, shift, scale, seq):
    n, d = x2.shape
    tm = _tile(seq, 512, BF16_ROWS)
    tpb = seq // tm
    vec = pl.BlockSpec((1, 1, d), lambda i: (i // tpb, 0, 0))
    return pl.pallas_call(
        _norm_mod_kernel,
        out_shape=jax.ShapeDtypeStruct((n, d), BF16),
        grid=(n // tm,),
        in_specs=[pl.BlockSpec((tm, d), lambda i: (i, 0)),
                  pl.BlockSpec((1, d), lambda i: (0, 0)), vec, vec],
        out_specs=pl.BlockSpec((tm, d), lambda i: (i, 0)),
        compiler_params=_params("arbitrary"),
        name="norm_mod",
    )(x2, g.reshape(1, d), shift, scale)


def _mm_kernel(a_ref, w_ref, *rest, glu, out_scale):
    *rest, wb_ref = rest

    @pl.when(pl.program_id(1) == 0)
    def _():
        wb_ref[...] = w_ref[...].astype(BF16)

    acc = jnp.dot(a_ref[...], wb_ref[...], preferred_element_type=F32)
    if glu:
        g_ref, b_ref, o_ref = rest
        o_ref[...] = (g_ref[...].astype(F32) * jax.nn.sigmoid(acc + b_ref[...])).astype(o_ref.dtype)
    else:
        (o_ref,) = rest
        if out_scale != 1.0:
            acc = acc * out_scale
        o_ref[...] = acc.astype(o_ref.dtype)


def _mm(a, w_stack, layer, nout, *, glu_bias=None, out_scale=1.0, name):
    n, k = a.shape
    tm = _tile(n, 1024, BF16_ROWS)
    tn = _tile(nout, 1024, LANES)
    if w_stack.ndim == 2:
        w_spec = pl.BlockSpec((k, tn), lambda j, i: (0, j))
    else:
        w_spec = pl.BlockSpec((None, k, tn), lambda j, i: (layer, 0, j))
    in_specs = [pl.BlockSpec((tm, k), lambda j, i: (i, 0)), w_spec]
    args = [a, w_stack]
    if glu_bias is not None:
        in_specs += [pl.BlockSpec((tm, tn), lambda j, i: (i, j)),
                     pl.BlockSpec((1, tn), lambda j, i: (0, j))]
        args += [a, glu_bias.reshape(1, nout)]
    return pl.pallas_call(
        functools.partial(_mm_kernel, glu=glu_bias is not None, out_scale=out_scale),
        out_shape=jax.ShapeDtypeStruct((n, nout), BF16),
        grid=(nout // tn, n // tm),
        in_specs=in_specs,
        out_specs=pl.BlockSpec((tm, tn), lambda j, i: (i, j)),
        scratch_shapes=[pltpu.VMEM((k, tn), BF16)],
        compiler_params=_params("arbitrary", "arbitrary"),
        name=name,
    )(*args)


def _norm_arg_count(kinds):
    return sum(3 if kd == "mod" else 1 for kd in kinds)


def _emit_norms(xn, kinds, norm_refs, out_refs):
    xhat = _rms_hat(xn)
    first = None
    p = 0
    for kd, o_ref in zip(kinds, out_refs):
        if kd == "mod":
            g_ref, sh_ref, sc_ref = norm_refs[p:p + 3]
            p += 3
            val = _modulate(xhat * g_ref[...], sh_ref[0], sc_ref[0])
        else:
            val = xhat * norm_refs[p][...]
            p += 1
        if first is None:
            first = val
        o_ref[...] = val.astype(o_ref.dtype)
    return first


def _norm_specs_args(kinds, norms, d, bidx):
    specs, args = [], []
    for kd, nm in zip(kinds, norms):
        specs.append(pl.BlockSpec((1, d), lambda *ix: (0, 0)))
        args.append(nm[0].reshape(1, d))
        if kd == "mod":
            for v in nm[1:]:
                specs.append(pl.BlockSpec((1, 1, d), bidx))
                args.append(v)
    return specs, args


def _norm_outs(kinds, n, d, tm, row_map):
    shapes = [jax.ShapeDtypeStruct((n, d), BF16 if kd == "mod" else F32) for kd in kinds]
    specs = [pl.BlockSpec((tm, d), row_map) for _ in kinds]
    return shapes, specs


def _mm_res_kernel(a_ref, w_ref, x_ref, gate_ref, *rest, kinds):
    *rest, wb_ref = rest
    na = _norm_arg_count(kinds)
    norm_refs, outs = rest[:na], rest[na:]

    @pl.when(pl.program_id(0) == 0)
    def _():
        wb_ref[...] = w_ref[...].astype(BF16)

    acc = jnp.dot(a_ref[...], wb_ref[...], preferred_element_type=F32)
    xn = x_ref[...] + gate_ref[0] * acc
    outs[0][...] = xn
    _emit_norms(xn, kinds, norm_refs, outs[1:])


def _mm_res(a, w_stack, layer, x2, gate, kinds, norms, seq, *, name):
    n, k = a.shape
    d = w_stack.shape[2]
    tm = _tile(seq, 512, BF16_ROWS)
    tpb = seq // tm
    bidx = lambda i: (i // tpb, 0, 0)
    row = lambda i: (i, 0)
    nspecs, nargs = _norm_specs_args(kinds, norms, d, bidx)
    oshapes, ospecs = _norm_outs(kinds, n, d, tm, row)
    return pl.pallas_call(
        functools.partial(_mm_res_kernel, kinds=tuple(kinds)),
        out_shape=[jax.ShapeDtypeStruct((n, d), F32)] + oshapes,
        grid=(n // tm,),
        in_specs=[pl.BlockSpec((tm, k), row),
                  pl.BlockSpec((None, k, d), lambda i: (layer, 0, 0), pipeline_mode=pl.Buffered(1)),
                  pl.BlockSpec((tm, d), row),
                  pl.BlockSpec((1, 1, d), bidx)] + nspecs,
        out_specs=[pl.BlockSpec((tm, d), row)] + ospecs,
        scratch_shapes=[pltpu.VMEM((k, d), BF16)],
        compiler_params=_params("arbitrary"),
        name=name,
    )(a, w_stack, x2, gate, *nargs)


def _ssm_kernel(u_ref, bp_ref, cp_ref, coef_ref, d_ref, o_ref, upad_ref, st_ref, ypad_ref,
                carry_ref, *, slabs, seg_len):
    s = STATE_LANES
    nseg = SUBLANES
    pitch = seg_len + 1
    nlt = s // LANES

    @pl.when(pl.program_id(2) == 0)
    def _():
        carry_ref[...] = jnp.zeros_like(carry_ref)

    def cmul(ar, ai, xr, xi):
        return ar * xr - ai * xi, ar * xi + ai * xr

    def load_rows(sb, half, rows):
        return jnp.concatenate([st_ref[sb, half * nlt + j, rows, :] for j in range(nlt)], axis=-1)

    def store_rows(sb, half, rows, val):
        for j in range(nlt):
            st_ref[sb, half * nlt + j, rows, :] = val[:, j * LANES:(j + 1) * LANES]

    for sb in range(slabs):
        ch = slice(sb * LANES, (sb + 1) * LANES)
        u32 = u_ref[:, ch].astype(F32)
        upad_ref[sb] = jnp.zeros(upad_ref.shape[1:], F32)
        for k in range(nseg):
            upad_ref[sb, pl.ds(k * pitch, seg_len), :] = u32[k * seg_len:(k + 1) * seg_len]
        bu = jnp.dot(upad_ref[sb].astype(BF16), bp_ref[sb], preferred_element_type=F32)
        for j in range(2 * nlt):
            st_ref[sb, j] = bu[:, j * LANES:(j + 1) * LANES]
        ar, ai = coef_ref[sb, 0], coef_ref[sb, 1]

        sr = jnp.zeros((nseg, s), F32)
        si = jnp.zeros((nseg, s), F32)
        for t in range(seg_len):
            rows = pl.ds(t, nseg, stride=pitch)
            pr_, pi_ = cmul(ar, ai, sr, si)
            sr = pr_ + load_rows(sb, 0, rows)
            si = pi_ + load_rows(sb, 1, rows)
            store_rows(sb, 0, rows, sr)
            store_rows(sb, 1, rows, si)

        tr, ti = sr, si
        for q, dlt in enumerate(SCAN_STEPS):
            hr, hi = coef_ref[sb, 2 + 2 * q], coef_ref[sb, 3 + 2 * q]
            dr, di = cmul(hr, hi, pltpu.roll(tr, dlt, 0), pltpu.roll(ti, dlt, 0))
            tr, ti = tr + dr, ti + di
        cr, ci = carry_ref[sb, 0], carry_ref[sb, 1]
        dr, di = cmul(coef_ref[sb, 8], coef_ref[sb, 9], cr, ci)
        tr, ti = tr + dr, ti + di
        row0 = lax.broadcasted_iota(jnp.int32, (nseg, s), 0) == 0
        zr = jnp.where(row0, cr, pltpu.roll(tr, 1, 0))
        zi = jnp.where(row0, ci, pltpu.roll(ti, 1, 0))
        carry_ref[sb, 0] = jnp.broadcast_to(tr[nseg - 1:nseg], tr.shape)
        carry_ref[sb, 1] = jnp.broadcast_to(ti[nseg - 1:nseg], ti.shape)

        for t in range(seg_len):
            rows = pl.ds(t, nseg, stride=pitch)
            zr, zi = cmul(ar, ai, zr, zi)
            store_rows(sb, 0, rows, load_rows(sb, 0, rows) + zr)
            store_rows(sb, 1, rows, load_rows(sb, 1, rows) + zi)

        states = jnp.concatenate([st_ref[sb, j] for j in range(2 * nlt)], axis=-1).astype(BF16)
        ypad_ref[sb] = jnp.dot(states, cp_ref[sb], preferred_element_type=F32)
        y = jnp.concatenate([ypad_ref[sb, pl.ds(k * pitch, seg_len), :] for k in range(nseg)], axis=0)
        y = y + d_ref[:, ch] * u32
        o_ref[:, ch] = jax.nn.gelu(y).astype(o_ref.dtype)


def _ssm_chunk(seq):
    tt = _tile(seq, SSM_CHUNK, SUBLANES * SUBLANES)
    return tt, tt // SUBLANES


def _s5_pack(log_step, a_re, a_im, b_re, b_im, c_re, c_im, seg_len):
    g, p = a_re.shape
    h = b_re.shape[-1]
    nblk = g // GROUPS_PER_BLOCK
    step = jnp.exp(log_step.astype(F32))[:, None]
    lam_re, lam_im = a_re.astype(F32), a_im.astype(F32)

    def power(k):
        mag = jnp.exp(k * lam_re * step)
        return mag * jnp.cos(k * lam_im * step), mag * jnp.sin(k * lam_im * step)

    abar_re, abar_im = power(1)
    den = lam_re * lam_re + lam_im * lam_im
    nr = abar_re - 1.0
    fr = (nr * lam_re + abar_im * lam_im) / den
    fi = (abar_im * lam_re - nr * lam_im) / den
    br, bi = b_re.astype(F32), b_im.astype(F32)
    bbar_re = fr[..., None] * br - fi[..., None] * bi
    bbar_im = fr[..., None] * bi + fi[..., None] * br
    eye = jnp.eye(GROUPS_PER_BLOCK, dtype=F32)

    def pack_b(bb):
        t = bb.reshape(nblk, GROUPS_PER_BLOCK, p, h)
        t = jnp.einsum("jgph,gk->jghkp", t, eye)
        return t.reshape(nblk, LANES, STATE_LANES)

    def pack_c(cc):
        t = cc.astype(F32).reshape(nblk, GROUPS_PER_BLOCK, h, p)
        t = jnp.einsum("jghp,gk->jgpkh", t, eye)
        return t.reshape(nblk, STATE_LANES, LANES)

    bpack = jnp.concatenate([pack_b(bbar_re), pack_b(bbar_im)], axis=-1).astype(BF16)
    cpack = jnp.concatenate([pack_c(c_re), -pack_c(c_im)], axis=1).astype(BF16)

    rows = jnp.arange(SUBLANES)

    def lanes(v):
        return v.reshape(nblk, 1, STATE_LANES)

    ones = jnp.ones((1, SUBLANES, 1), F32)
    coefs = [lanes(abar_re) * ones, lanes(abar_im) * ones]
    for dlt in SCAN_STEPS:
        keep = (rows >= dlt).astype(F32)[None, :, None]
        pre, pim = power(float(seg_len * dlt))
        coefs += [lanes(pre) * keep, lanes(pim) * keep]
    pw = [power(float(seg_len * (r + 1))) for r in range(SUBLANES)]
    coefs.append(jnp.stack([lanes(q[0])[:, 0] for q in pw], axis=1))
    coefs.append(jnp.stack([lanes(q[1])[:, 0] for q in pw], axis=1))
    coef = jnp.stack(coefs, axis=1)
    return bpack, cpack, coef


def _ssm(u, bpack, cpack, coef, dvec, bsz, seq):
    n, d = u.shape
    nblk = d // LANES
    tt, seg_len = _ssm_chunk(seq)
    nt = seq // tt
    slabs = next(c for c in (SSM_SLABS, 2, 1) if nblk % c == 0)
    width = slabs * LANES
    prow =-(-(SUBLANES * (seg_len + 1)) // BF16_ROWS) * BF16_ROWS
    row = lambda b, j, t: (b * nt + t, j)
    return pl.pallas_call(
        functools.partial(_ssm_kernel, slabs=slabs, seg_len=seg_len),
        out_shape=jax.ShapeDtypeStruct((n, d), BF16),
        grid=(bsz, nblk // slabs, nt),
        in_specs=[pl.BlockSpec((tt, width), row),
                  pl.BlockSpec((slabs, LANES, 2 * STATE_LANES), lambda b, j, t: (j, 0, 0)),
                  pl.BlockSpec((slabs, 2 * STATE_LANES, LANES), lambda b, j, t: (j, 0, 0)),
                  pl.BlockSpec((slabs, 2 * len(SCAN_STEPS) + 4, SUBLANES, STATE_LANES),
                               lambda b, j, t: (j, 0, 0, 0)),
                  pl.BlockSpec((1, width), lambda b, j, t: (0, j))],
        out_specs=pl.BlockSpec((tt, width), row),
        scratch_shapes=[pltpu.VMEM((slabs, prow, LANES), F32),
                        pltpu.VMEM((slabs, 2 * STATE_LANES // LANES, prow, LANES), F32),
                        pltpu.VMEM((slabs, prow, LANES), F32),
                        pltpu.VMEM((slabs, 2, SUBLANES, STATE_LANES), F32)],
        compiler_params=_params("arbitrary", "arbitrary", "arbitrary"),
        name="s5_ssm",
    )(u, bpack, cpack, coef, dvec.reshape(1, d))


def _ffn_up_kernel(h_ref, wu_ref, wv_ref, cwu_ref, cwv_ref, cbu_ref, cbv_ref, o_ref,
                   wub_ref, wvb_ref, carry_ref, *, tiles_per_batch):
    i = pl.program_id(1)
    tm = h_ref.shape[0]

    @pl.when(i == 0)
    def _():
        wub_ref[...] = wu_ref[...].astype(BF16)
        wvb_ref[...] = wv_ref[...].astype(BF16)

    first = (i % tiles_per_batch) == 0
    h = h_ref[...]
    conv = []
    for stream, (wb_ref, cw_ref, cb_ref) in enumerate(((wub_ref, cwu_ref, cbu_ref),
                                                        (wvb_ref, cwv_ref, cbv_ref))):
        a = jnp.dot(h, wb_ref[...], preferred_element_type=F32)
        pv = carry_ref[stream]
        prev = jnp.where(first, jnp.zeros_like(pv), pv)
        carry_ref[stream] = a[tm - SUBLANES:]
        ext = jnp.concatenate([prev, a], axis=0)
        cw = cw_ref[...]
        out = cw[CONV_WIDTH - 1:CONV_WIDTH] * a
        for lag in range(1, CONV_WIDTH):
            tap = CONV_WIDTH - 1 - lag
            out = out + cw[tap:tap + 1] * pltpu.roll(ext, lag, 0)[SUBLANES:]
        conv.append(out + cb_ref[...])
    cu, cv = conv
    o_ref[...] = (cu * jax.nn.sigmoid(cu) * cv).astype(o_ref.dtype)


def _ffn_up(h, w_up, conv_w, conv_b, layer, seq):
    n, d = h.shape
    ff = w_up.shape[2] // 2
    tm = _tile(seq, 1024, BF16_ROWS)
    tf = _tile(ff, 512, LANES)
    nf = ff // tf
    cb = conv_b.astype(F32).reshape(conv_b.shape[0], 1, 2 * ff)
    return pl.pallas_call(
        functools.partial(_ffn_up_kernel, tiles_per_batch=seq // tm),
        out_shape=jax.ShapeDtypeStruct((n, ff), BF16),
        grid=(nf, n // tm),
        in_specs=[pl.BlockSpec((tm, d), lambda f, i: (i, 0)),
                  pl.BlockSpec((None, d, tf), lambda f, i: (layer, 0, f)),
                  pl.BlockSpec((None, d, tf), lambda f, i: (layer, 0, nf + f)),
                  pl.BlockSpec((None, CONV_WIDTH, tf), lambda f, i: (layer, 0, f)),
                  pl.BlockSpec((None, CONV_WIDTH, tf), lambda f, i: (layer, 0, nf + f)),
                  pl.BlockSpec((None, 1, tf), lambda f, i: (layer, 0, f)),
                  pl.BlockSpec((None, 1, tf), lambda f, i: (layer, 0, nf + f))],
        out_specs=pl.BlockSpec((tm, tf), lambda f, i: (i, f)),
        scratch_shapes=[pltpu.VMEM((d, tf), BF16), pltpu.VMEM((d, tf), BF16),
                        pltpu.VMEM((2, SUBLANES, tf), F32)],
        compiler_params=_params("arbitrary", "arbitrary"),
        name="ffn_up_conv",
    )(h, w_up, w_up, conv_w.astype(F32), conv_w.astype(F32), cb, cb)


def _ffn_down_kernel(a_ref, w_ref, x_ref, gate_ref, *rest, kinds, emit_x, with_flog):
    na = _norm_arg_count(kinds)
    norm_refs = rest[:na]
    rest = rest[na:]
    if with_flog:
        wf_ref, rest = rest[0], rest[1:]
    outs = rest
    acc_ref = outs[0]
    k = pl.program_id(1)

    def contrib():
        return jnp.dot(a_ref[...], w_ref[...], preferred_element_type=F32)

    @pl.when(k == 0)
    def _():
        acc_ref[...] = contrib()

    @pl.when(k > 0)
    def _():
        acc_ref[...] += contrib()

    @pl.when(k == pl.num_programs(1) - 1)
    def _():
        xn = x_ref[...] + gate_ref[0] * acc_ref[...]
        q = 0
        if emit_x:
            outs[0][...] = xn
            q = 1
        first = _emit_norms(xn, kinds, norm_refs, outs[q:q + len(kinds)])
        if with_flog:
            hi = first.astype(BF16)
            lo = (first - hi.astype(F32)).astype(BF16)
            wf = wf_ref[...]
            whi = wf.astype(BF16)
            wlo = (wf - whi.astype(F32)).astype(BF16)
            nt = (((1,), (1,)), ((), ()))
            nh = wf.shape[0]
            both = lax.dot_general(jnp.concatenate([whi, wlo], axis=0), hi, nt,
                                   preferred_element_type=F32)
            outs[-1][...] = (both[:nh] + both[nh:]
                             + lax.dot_general(whi, lo, nt, preferred_element_type=F32))


def _ffn_down(act, x2, w_down, layer, gate, kinds, norms, seq, *, emit_x, wf_t=None):
    n, ff = act.shape
    d = w_down.shape[2]
    tm = _tile(seq, 512, BF16_ROWS)
    tk = _tile(ff, 1536, LANES)
    tpb = seq // tm
    row = lambda i, k: (i, 0)
    bidx = lambda i, k: (i // tpb, 0, 0)
    nspecs, nargs = _norm_specs_args(kinds, norms, d, bidx)
    oshapes, ospecs = _norm_outs(kinds, n, d, tm, row)
    if emit_x:
        oshapes = [jax.ShapeDtypeStruct((n, d), F32)] + oshapes
        ospecs = [pl.BlockSpec((tm, d), row)] + ospecs
    assert oshapes[0].dtype == F32
    extra_specs, extra_args = [], []
    if wf_t is not None:
        nh = wf_t.shape[0]
        extra_specs = [pl.BlockSpec((nh, d), lambda i, k: (0, 0))]
        extra_args = [wf_t]
        oshapes = oshapes + [jax.ShapeDtypeStruct((nh, n), F32)]
        ospecs = ospecs + [pl.BlockSpec((nh, tm), lambda i, k: (0, i))]
    return pl.pallas_call(
        functools.partial(_ffn_down_kernel, kinds=tuple(kinds), emit_x=emit_x,
                          with_flog=wf_t is not None),
        out_shape=oshapes,
        grid=(n // tm, ff // tk),
        in_specs=[pl.BlockSpec((tm, tk), lambda i, k: (i, k)),
                  pl.BlockSpec((None, tk, d), lambda i, k: (layer, k, 0)),
                  pl.BlockSpec((tm, d), row),
                  pl.BlockSpec((1, 1, d), bidx)] + nspecs + extra_specs,
        out_specs=ospecs,
        compiler_params=_params("arbitrary", "arbitrary"),
        name="ffn_down_res",
    )(act, w_down, x2, gate, *nargs, *extra_args)


def _fcum_kernel(z_ref, b_ref, tri_ref, o_ref):
    z = z_ref[...] + b_ref[...]
    lf = jnp.minimum(z, 0.0) - jnp.log1p(jnp.exp(-jnp.abs(z)))
    carry = jnp.zeros((z.shape[0], 1), F32)
    for cidx in range(z.shape[1] // LANES):
        sl = slice(cidx * LANES, (cidx + 1) * LANES)
        cs = jnp.dot(lf[:, sl], tri_ref[...], precision=lax.Precision.HIGHEST,
                     preferred_element_type=F32) + carry
        o_ref[:, sl] = cs * LOG2E
        carry = cs[:, LANES - 1:LANES]


def _fcum(flog_t, forget_b, bsz, seq):
    nh, n = flog_t.shape
    tri = (jnp.arange(LANES)[:, None] <= jnp.arange(LANES)[None, :]).astype(F32)
    return pl.pallas_call(
        _fcum_kernel,
        out_shape=jax.ShapeDtypeStruct((nh, n), F32),
        grid=(bsz,),
        in_specs=[pl.BlockSpec((nh, seq), lambda b: (0, b)),
                  pl.BlockSpec((nh, 1), lambda b: (0, 0)),
                  pl.BlockSpec((LANES, LANES), lambda b: (0, 0))],
        out_specs=pl.BlockSpec((nh, seq), lambda b: (0, b)),
        compiler_params=_params("arbitrary"),
        name="forget_cumsum",
    )(flog_t, forget_b.astype(F32).reshape(nh, 1), tri)


def _attn_kernel(q_ref, k_ref, v_ref, f_ref, mask_ref, o_ref, m_ref, l_ref, acc_ref, *, tk, heads):
    qi = pl.program_id(2)
    m_ref[...] = jnp.full_like(m_ref, -jnp.inf)
    l_ref[...] = jnp.zeros_like(l_ref)
    acc_ref[...] = jnp.zeros_like(acc_ref)
    ncol = tk // LANES

    def head_step(hd, cidx, masked):
        hs = slice(hd * HEAD_DIM, (hd + 1) * HEAD_DIM)
        k0 = pl.multiple_of(cidx * tk, tk)
        kc = k_ref[pl.ds(k0, tk), hs]
        vc = v_ref[pl.ds(k0, tk), hs]
        s = lax.dot_general(q_ref[:, hs], kc, (((1,), (1,)), ((), ())), preferred_element_type=F32)
        s = s - f_ref[hd, :, pl.ds(k0, tk)]
        if masked:
            s = s + mask_ref[...]
        cols_s = [s[:, j * LANES:(j + 1) * LANES] for j in range(ncol)]
        mc = cols_s[0]
        for sj in cols_s[1:]:
            mc = jnp.maximum(mc, sj)
        m_old = m_ref[hd]
        m_new = jnp.maximum(m_old, jnp.max(mc, axis=-1, keepdims=True))
        alpha = jnp.exp2(m_old - m_new)
        ps = [jnp.exp2(sj - m_new) for sj in cols_s]
        lsum = ps[0]
        for pj in ps[1:]:
            lsum = lsum + pj
        l_ref[hd] = alpha * l_ref[hd] + lsum
        p = jnp.concatenate(ps, axis=-1).astype(BF16)
        acc_ref[hd] = alpha * acc_ref[hd] + jnp.dot(p, vc, preferred_element_type=F32)
        m_ref[hd] = m_new

    def step(cidx, masked):
        for hd in range(heads):
            head_step(hd, cidx, masked)

    n_full = qi

    def pair_body(pidx, carry):
        step(2 * pidx, False)
        step(2 * pidx + 1, False)
        return carry

    lax.fori_loop(0, n_full // 2, pair_body, 0)

    @pl.when(n_full % 2 == 1)
    def _():
        step(n_full - 1, False)

    step(n_full, True)
    for hd in range(heads):
        l_row = jnp.sum(l_ref[hd], axis=-1, keepdims=True)
        o_ref[:, hd * HEAD_DIM:(hd + 1) * HEAD_DIM] = (acc_ref[hd] / l_row).astype(o_ref.dtype)


def _attention(q, kv, f_cum, bsz, seq):
    n, d = q.shape
    nh = d // HEAD_DIM
    heads = next(c for c in (ATTN_HEADS, 2, 1) if nh % c == 0)
    tq = _tile(seq, 512, LANES)
    tk = tq
    nq = seq // tq
    width = heads * HEAD_DIM
    ngrp = nh // heads
    causal = jnp.where(jnp.arange(tk)[None, :] <= jnp.arange(tq)[:, None], 0.0, -jnp.inf).astype(F32)
    return pl.pallas_call(
        functools.partial(_attn_kernel, tk=tk, heads=heads),
        out_shape=jax.ShapeDtypeStruct((n, d), BF16),
        grid=(bsz, ngrp, nq),
        in_specs=[pl.BlockSpec((tq, width), lambda b, h, i: (b * nq + i, h)),
                  pl.BlockSpec((seq, width), lambda b, h, i: (b, h)),
                  pl.BlockSpec((seq, width), lambda b, h, i: (b, ngrp + h)),
                  pl.BlockSpec((heads, 1, seq), lambda b, h, i: (h, 0, b)),
                  pl.BlockSpec((tq, tk), lambda b, h, i: (0, 0))],
        out_specs=pl.BlockSpec((tq, width), lambda b, h, i: (b * nq + i, h)),
        scratch_shapes=[pltpu.VMEM((heads, tq, LANES), F32), pltpu.VMEM((heads, tq, LANES), F32),
                        pltpu.VMEM((heads, tq, HEAD_DIM), F32)],
        compiler_params=_params("arbitrary", "arbitrary", "arbitrary"),
        name="fox_attention",
    )(q, kv, kv, f_cum.reshape(nh, 1, n), causal)


def kernel(x, c, mod_w, mod_b, norm_mix_g, norm_ffn_g, ssm_w_in, ssm_log_step, ssm_a_re, ssm_a_im,
           ssm_b_re, ssm_b_im, ssm_c_re, ssm_c_im, ssm_d, ssm_glu_w, ssm_glu_b, ssm_w_out,
           kv_mod_w, kv_mod_b, kv_norm_g, kv_w, forget_b, attn_w_q, attn_w_out,
           ffn_w_up, ffn_conv_w, ffn_conv_b, ffn_w_down, final_norm_g):
    bsz, seq, d = x.shape
    n = bsz * seq
    depth = mod_w.shape[0]
    n_a = ssm_w_in.shape[0]
    assert n_a >= 1 and d % LANES == 0
    x2 = x.astype(F32).reshape(n, d)

    c_rep = jnp.broadcast_to(c.astype(F32)[:, :, None], (bsz, d, LANES))

    def vecs(m, count):
        return [m[:, k * d:(k + 1) * d].reshape(bsz, 1, d) for k in range(count)]

    mods = [vecs(_gemv(c_rep, mod_w, i, mod_b[i]), N_MOD) for i in range(depth)]
    sh_kv, sc_kv = vecs(_gemv(c_rep, kv_mod_w.reshape(1, *kv_mod_w.shape), 0, kv_mod_b), 2)

    def mix_norm(i):
        return (norm_mix_g[i], mods[i][0], mods[i][1])

    def after_ffn(i):
        if i + 1 == depth:
            return ["plain"], [(final_norm_g,)]
        kinds, norms = [], []
        if i + 1 == n_a:
            kinds.append("mod")
            norms.append((kv_norm_g, sh_kv, sc_kv))
        kinds.append("mod")
        norms.append(mix_norm(i + 1))
        return kinds, norms

    h = _norm_mod(x2, *mix_norm(0), seq)
    w_down_bf = ffn_w_down.astype(BF16)
    kv = f_cum = None
    hk = flog_t = None
    out = None
    for i in range(depth):
        sh_m, sc_m, g_m, sh_f, sc_f, g_f = mods[i]
        ffn_norm = (norm_ffn_g[i], sh_f, sc_f)
        if i == n_a:
            kv = _mm(hk, kv_w[:, :2 * d], 0, 2 * d, name="kv_proj")
            f_cum = _fcum(flog_t, forget_b, bsz, seq)
        if i < n_a:
            u = _mm(h, ssm_w_in, i, d, name="ssm_in_proj")
            bpack, cpack, coef = _s5_pack(ssm_log_step[i], ssm_a_re[i], ssm_a_im[i], ssm_b_re[i],
                                          ssm_b_im[i], ssm_c_re[i], ssm_c_im[i], _ssm_chunk(seq)[1])
            g = _ssm(u, bpack, cpack, coef, ssm_d[i].astype(F32), bsz, seq)
            z = _mm(g, ssm_glu_w, i, d, glu_bias=ssm_glu_b[i].astype(F32), name="ssm_glu")
            x2, h = _mm_res(z, ssm_w_out, i, x2, g_m, ["mod"], [ffn_norm], seq, name="ssm_out_proj")
        else:
            j = i - n_a
            q = _mm(h, attn_w_q, j, d, out_scale=HEAD_DIM ** -0.5 * LOG2E, name="q_proj")
            o = _attention(q, kv, f_cum, bsz, seq)
            x2, h = _mm_res(o, attn_w_out, j, x2, g_m, ["mod"], [ffn_norm], seq, name="attn_out_proj")
        kinds, norms = after_ffn(i)
        last = i + 1 == depth
        want_kv = (i + 1 == n_a) and not last
        wf_t = kv_w[:, 2 * d:].astype(F32).T if want_kv else None
        act = _ffn_up(h, ffn_w_up, ffn_conv_w, ffn_conv_b, i, seq)
        res = _ffn_down(act, x2, w_down_bf, i, g_f, kinds, norms, seq, emit_x=not last, wf_t=wf_t)
        if last:
            out = res[0]
        else:
            x2 = res[0]
            if want_kv:
                hk, h, flog_t = res[1], res[2], res[3]
            else:
                h = res[1]
    return out.reshape(bsz, seq, d).astype(x.dtype)
```

```python
import functools
import math

import jax
import jax.numpy as jnp
from jax import lax
from jax.experimental import pallas as pl
from jax.experimental.pallas import tpu as pltpu

F32 = jnp.float32
BF16 = jnp.bfloat16

LANES = 128
SUBLANES = 8
BF16_ROWS = 16
VMEM_LIMIT_BYTES = 56 * 1024 * 1024

SSM_GROUP = 16
SSM_STATE = 64
HEAD_DIM = 128
CONV_WIDTH = 3
NORM_EPS = 1e-6
LOG2E = math.log2(math.e)
N_MOD = 6

GROUPS_PER_BLOCK = LANES // SSM_GROUP
STATE_LANES = GROUPS_PER_BLOCK * SSM_STATE
SCAN_STEPS = (1, 2, 4)
SSM_CHUNK = 512
SSM_SLABS = 8
ATTN_HEADS = 4

def _params(*sem):
    return pltpu.CompilerParams(dimension_semantics=sem, vmem_limit_bytes=VMEM_LIMIT_BYTES)


def _tile(n, pref, align):
    if n <= pref:
        return n
    t = (pref // align) * align
    while t >= align:
        if n % t == 0:
            return t
        t -= align
    raise ValueError(f"no tile for {n}")


def _rms_hat(x):
    ms = jnp.mean(x * x, axis=-1, keepdims=True)
    return x * lax.rsqrt(ms + NORM_EPS)


def _modulate(y, shift, scale):
    return y * (1 + scale) + shift


def _gemv_kernel(c_ref, w_ref, b_ref, o_ref, cs_ref):
    @pl.when(pl.program_id(0) == 0)
    def _():
        cv = c_ref[...]
        cs_ref[...] = cv * jax.nn.sigmoid(cv)

    tn = w_ref.shape[1]
    for b in range(c_ref.shape[0]):
        cb = cs_ref[b]
        for lt in range(tn // LANES):
            sl = slice(lt * LANES, (lt + 1) * LANES)
            o_ref[b:b + 1, sl] = jnp.sum(w_ref[:, sl] * cb, axis=0, keepdims=True) + b_ref[:, sl]


def _gemv(c_rep, w_stack, layer, b):
    bsz, k, _ = c_rep.shape
    n = w_stack.shape[2]
    tn = _tile(n, 1024, LANES)
    return pl.pallas_call(
        _gemv_kernel,
        out_shape=jax.ShapeDtypeStruct((bsz, n), F32),
        grid=(n // tn,),
        in_specs=[pl.BlockSpec((bsz, k, LANES), lambda j: (0, 0, 0)),
                  pl.BlockSpec((None, k, tn), lambda j: (layer, 0, j)),
                  pl.BlockSpec((1, tn), lambda j: (0, j))],
        out_specs=pl.BlockSpec((bsz, tn), lambda j: (0, j)),
        scratch_shapes=[pltpu.VMEM((bsz, k, LANES), F32)],
        compiler_params=_params("arbitrary"),
        name="adaln_gemv",
    )(c_rep, w_stack, b.reshape(1, n))


def _norm_mod_kernel(x_ref, g_ref, sh_ref, sc_ref, o_ref):
    y = _rms_hat(x_ref[...]) * g_ref[...]
    o_ref[...] = _modulate(y, sh_ref[0], sc_ref[0]).astype(o_ref.dtype)


def _norm_mod(x2, g, shift, scale, seq):
    n, d = x2.shape
    tm = _tile(seq, 512, BF16_ROWS)
    tpb = seq // tm
    vec = pl.BlockSpec((1, 1, d), lambda i: (i // tpb, 0, 0))
    return pl.pallas_call(
        _norm_mod_kernel,
        out_shape=jax.ShapeDtypeStruct((n, d), BF16),
        grid=(n // tm,),
        in_specs=[pl.BlockSpec((tm, d), lambda i: (i, 0)),
                  pl.BlockSpec((1, d), lambda i: (0, 0)), vec, vec],
        out_specs=pl.BlockSpec((tm, d), lambda i: (i, 0)),
        compiler_params=_params("arbitrary"),
        name="norm_mod",
    )(x2, g.reshape(1, d), shift, scale)


def _mm_kernel(a_ref, w_ref, *rest, glu, out_scale):
    *rest, wb_ref = rest

    @pl.when(pl.program_id(1) == 0)
    def _():
        wb_ref[...] = w_ref[...].astype(BF16)

    acc = jnp.dot(a_ref[...], wb_ref[...], preferred_element_type=F32)
    if glu:
        g_ref, b_ref, o_ref = rest
        o_ref[...] = (g_ref[...].astype(F32) * jax.nn.sigmoid(acc + b_ref[...])).astype(o_ref.dtype)
    else:
        (o_ref,) = rest
        if out_scale != 1.0:
            acc = acc * out_scale
        o_ref[...] = acc.astype(o_ref.dtype)


def _mm(a, w_stack, layer, nout, *, glu_bias=None, out_scale=1.0, name):
    n, k = a.shape
    tm = _tile(n, 1024, BF16_ROWS)
    tn = _tile(nout, 1024, LANES)
    if w_stack.ndim == 2:
        w_spec = pl.BlockSpec((k, tn), lambda j, i: (0, j))
    else:
        w_spec = pl.BlockSpec((None, k, tn), lambda j, i: (layer, 0, j))
    in_specs = [pl.BlockSpec((tm, k), lambda j, i: (i, 0)), w_spec]
    args = [a, w_stack]
    if glu_bias is not None:
        in_specs += [pl.BlockSpec((tm, tn), lambda j, i: (i, j)),
                     pl.BlockSpec((1, tn), lambda j, i: (0, j))]
        args += [a, glu_bias.reshape(1, nout)]
    return pl.pallas_call(
        functools.partial(_mm_kernel, glu=glu_bias is not None, out_scale=out_scale),
        out_shape=jax.ShapeDtypeStruct((n, nout), BF16),
        grid=(nout // tn, n // tm),
        in_specs=in_specs,
        out_specs=pl.BlockSpec((tm, tn), lambda j, i: (i, j)),
        scratch_shapes=[pltpu.VMEM((k, tn), BF16)],
        compiler_params=_params("arbitrary", "arbitrary"),
        name=name,
    )(*args)


def _norm_arg_count(kinds):
    return sum(3 if kd == "mod" else 1 for kd in kinds)


def _emit_norms(xn, kinds, norm_refs, out_refs):
    xhat = _rms_hat(xn)
    first = None
    p = 0
    for kd, o_ref in zip(kinds, out_refs):
        if kd == "mod":
            g_ref, sh_ref, sc_ref = norm_refs[p:p + 3]
            p += 3
            val = _modulate(xhat * g_ref[...], sh_ref[0], sc_ref[0])
        else:
            val = xhat * norm_refs[p][...]
            p += 1
        if first is None:
            first = val
        o_ref[...] = val.astype(o_ref.dtype)
    return first


def _norm_specs_args(kinds, norms, d, bidx):
    specs, args = [], []
    for kd, nm in zip(kinds, norms):
        specs.append(pl.BlockSpec((1, d), lambda *ix: (0, 0)))
        args.append(nm[0].reshape(1, d))
        if kd == "mod":
            for v in nm[1:]:
                specs.append(pl.BlockSpec((1, 1, d), bidx))
                args.append(v)
    return specs, args


def _norm_outs(kinds, n, d, tm, row_map):
    shapes = [jax.ShapeDtypeStruct((n, d), BF16 if kd == "mod" else F32) for kd in kinds]
    specs = [pl.BlockSpec((tm, d), row_map) for _ in kinds]
    return shapes, specs


def _mm_res_kernel(a_ref, w_ref, x_ref, gate_ref, *rest, kinds):
    *rest, wb_ref = rest
    na = _norm_arg_count(kinds)
    norm_refs, outs = rest[:na], rest[na:]

    @pl.when(pl.program_id(0) == 0)
    def _():
        wb_ref[...] = w_ref[...].astype(BF16)

    acc = jnp.dot(a_ref[...], wb_ref[...], preferred_element_type=F32)
    xn = x_ref[...] + gate_ref[0] * acc
    outs[0][...] = xn
    _emit_norms(xn, kinds, norm_refs, outs[1:])


def _mm_res(a, w_stack, layer, x2, gate, kinds, norms, seq, *, name):
    n, k = a.shape
    d = w_stack.shape[2]
    tm = _tile(seq, 512, BF16_ROWS)
    tpb = seq // tm
    bidx = lambda i: (i // tpb, 0, 0)
    row = lambda i: (i, 0)
    nspecs, nargs = _norm_specs_args(kinds, norms, d, bidx)
    oshapes, ospecs = _norm_outs(kinds, n, d, tm, row)
    return pl.pallas_call(
        functools.partial(_mm_res_kernel, kinds=tuple(kinds)),
        out_shape=[jax.ShapeDtypeStruct((n, d), F32)] + oshapes,
        grid=(n // tm,),
        in_specs=[pl.BlockSpec((tm, k), row),
                  pl.BlockSpec((None, k, d), lambda i: (layer, 0, 0), pipeline_mode=pl.Buffered(1)),
                  pl.BlockSpec((tm, d), row),
                  pl.BlockSpec((1, 1, d), bidx)] + nspecs,
        out_specs=[pl.BlockSpec((tm, d), row)] + ospecs,
        scratch_shapes=[pltpu.VMEM((k, d), BF16)],
        compiler_params=_params("arbitrary"),
        name=name,
    )(a, w_stack, x2, gate, *nargs)


def _ssm_kernel(u_ref, bp_ref, cp_ref, coef_ref, d_ref, o_ref, upad_ref, st_ref, ypad_ref,
                carry_ref, *, slabs, seg_len):
    s = STATE_LANES
    nseg = SUBLANES
    pitch = seg_len + 1
    nlt = s // LANES

    @pl.when(pl.program_id(2) == 0)
    def _():
        carry_ref[...] = jnp.zeros_like(carry_ref)

    def cmul(ar, ai, xr, xi):
        return ar * xr - ai * xi, ar * xi + ai * xr

    def load_rows(sb, half, rows):
        return jnp.concatenate([st_ref[sb, half * nlt + j, rows, :] for j in range(nlt)], axis=-1)

    def store_rows(sb, half, rows, val):
        for j in range(nlt):
            st_ref[sb, half * nlt + j, rows, :] = val[:, j * LANES:(j + 1) * LANES]

    for sb in range(slabs):
        ch = slice(sb * LANES, (sb + 1) * LANES)
        u32 = u_ref[:, ch].astype(F32)
        upad_ref[sb] = jnp.zeros(upad_ref.shape[1:], F32)
        for k in range(nseg):
            upad_ref[sb, pl.ds(k * pitch, seg_len), :] = u32[k * seg_len:(k + 1) * seg_len]
        bu = jnp.dot(upad_ref[sb].astype(BF16), bp_ref[sb], preferred_element_type=F32)
        for j in range(2 * nlt):
            st_ref[sb, j] = bu[:, j * LANES:(j + 1) * LANES]
        ar, ai = coef_ref[sb, 0], coef_ref[sb, 1]

        sr = jnp.zeros((nseg, s), F32)
        si = jnp.zeros((nseg, s), F32)
        for t in range(seg_len):
            rows = pl.ds(t, nseg, stride=pitch)
            pr_, pi_ = cmul(ar, ai, sr, si)
            sr = pr_ + load_rows(sb, 0, rows)
            si = pi_ + load_rows(sb, 1, rows)
            store_rows(sb, 0, rows, sr)
            store_rows(sb, 1, rows, si)

        tr, ti = sr, si
        for q, dlt in enumerate(SCAN_STEPS):
            hr, hi = coef_ref[sb, 2 + 2 * q], coef_ref[sb, 3 + 2 * q]
            dr, di = cmul(hr, hi, pltpu.roll(tr, dlt, 0), pltpu.roll(ti, dlt, 0))
            tr, ti = tr + dr, ti + di
        cr, ci = carry_ref[sb, 0], carry_ref[sb, 1]
        dr, di = cmul(coef_ref[sb, 8], coef_ref[sb, 9], cr, ci)
        tr, ti = tr + dr, ti + di
        row0 = lax.broadcasted_iota(jnp.int32, (nseg, s), 0) == 0
        zr = jnp.where(row0, cr, pltpu.roll(tr, 1, 0))
        zi = jnp.where(row0, ci, pltpu.roll(ti, 1, 0))
        carry_ref[sb, 0] = jnp.broadcast_to(tr[nseg - 1:nseg], tr.shape)
        carry_ref[sb, 1] = jnp.broadcast_to(ti[nseg - 1:nseg], ti.shape)

        for t in range(seg_len):
            rows = pl.ds(t, nseg, stride=pitch)
            zr, zi = cmul(ar, ai, zr, zi)
            store_rows(sb, 0, rows, load_rows(sb, 0, rows) + zr)
            store_rows(sb, 1, rows, load_rows(sb, 1, rows) + zi)

        states = jnp.concatenate([st_ref[sb, j] for j in range(2 * nlt)], axis=-1).astype(BF16)
        ypad_ref[sb] = jnp.dot(states, cp_ref[sb], preferred_element_type=F32)
        y = jnp.concatenate([ypad_ref[sb, pl.ds(k * pitch, seg_len), :] for k in range(nseg)], axis=0)
        y = y + d_ref[:, ch] * u32
        o_ref[:, ch] = jax.nn.gelu(y).astype(o_ref.dtype)


def _ssm_chunk(seq):
    tt = _tile(seq, SSM_CHUNK, SUBLANES * SUBLANES)
    return tt, tt // SUBLANES


def _s5_pack(log_step, a_re, a_im, b_re, b_im, c_re, c_im, seg_len):
    g, p = a_re.shape
    h = b_re.shape[-1]
    nblk = g // GROUPS_PER_BLOCK
    step = jnp.exp(log_step.astype(F32))[:, None]
    lam_re, lam_im = a_re.astype(F32), a_im.astype(F32)

    def power(k):
        mag = jnp.exp(k * lam_re * step)
        return mag * jnp.cos(k * lam_im * step), mag * jnp.sin(k * lam_im * step)

    abar_re, abar_im = power(1)
    den = lam_re * lam_re + lam_im * lam_im
    nr = abar_re - 1.0
    fr = (nr * lam_re + abar_im * lam_im) / den
    fi = (abar_im * lam_re - nr * lam_im) / den
    br, bi = b_re.astype(F32), b_im.astype(F32)
    bbar_re = fr[..., None] * br - fi[..., None] * bi
    bbar_im = fr[..., None] * bi + fi[..., None] * br
    eye = jnp.eye(GROUPS_PER_BLOCK, dtype=F32)

    def pack_b(bb):
        t = bb.reshape(nblk, GROUPS_PER_BLOCK, p, h)
        t = jnp.einsum("jgph,gk->jghkp", t, eye)
        return t.reshape(nblk, LANES, STATE_LANES)

    def pack_c(cc):
        t = cc.astype(F32).reshape(nblk, GROUPS_PER_BLOCK, h, p)
        t = jnp.einsum("jghp,gk->jgpkh", t, eye)
        return t.reshape(nblk, STATE_LANES, LANES)

    bpack = jnp.concatenate([pack_b(bbar_re), pack_b(bbar_im)], axis=-1).astype(BF16)
    cpack = jnp.concatenate([pack_c(c_re), -pack_c(c_im)], axis=1).astype(BF16)

    rows = jnp.arange(SUBLANES)

    def lanes(v):
        return v.reshape(nblk, 1, STATE_LANES)

    ones = jnp.ones((1, SUBLANES, 1), F32)
    coefs = [lanes(abar_re) * ones, lanes(abar_im) * ones]
    for dlt in SCAN_STEPS:
        keep = (rows >= dlt).astype(F32)[None, :, None]
        pre, pim = power(float(seg_len * dlt))
        coefs += [lanes(pre) * keep, lanes(pim) * keep]
    pw = [power(float(seg_len * (r + 1))) for r in range(SUBLANES)]
    coefs.append(jnp.stack([lanes(q[0])[:, 0] for q in pw], axis=1))
    coefs.append(jnp.stack([lanes(q[1])[:, 0] for q in pw], axis=1))
    coef = jnp.stack(coefs, axis=1)
    return bpack, cpack, coef


def _ssm(u, bpack, cpack, coef, dvec, bsz, seq):
    n, d = u.shape
    nblk = d // LANES
    tt, seg_len = _ssm_chunk(seq)
    nt = seq // tt
    slabs = next(c for c in (SSM_SLABS, 2, 1) if nblk % c == 0)
    width = slabs * LANES
    prow =-(-(SUBLANES * (seg_len + 1)) // BF16_ROWS) * BF16_ROWS
    row = lambda b, j, t: (b * nt + t, j)
    return pl.pallas_call(
        functools.partial(_ssm_kernel, slabs=slabs, seg_len=seg_len),
        out_shape=jax.ShapeDtypeStruct((n, d), BF16),
        grid=(bsz, nblk // slabs, nt),
        in_specs=[pl.BlockSpec((tt, width), row),
                  pl.BlockSpec((slabs, LANES, 2 * STATE_LANES), lambda b, j, t: (j, 0, 0)),
                  pl.BlockSpec((slabs, 2 * STATE_LANES, LANES), lambda b, j, t: (j, 0, 0)),
                  pl.BlockSpec((slabs, 2 * len(SCAN_STEPS) + 4, SUBLANES, STATE_LANES),
                               lambda b, j, t: (j, 0, 0, 0)),
                  pl.BlockSpec((1, width), lambda b, j, t: (0, j))],
        out_specs=pl.BlockSpec((tt, width), row),
        scratch_shapes=[pltpu.VMEM((slabs, prow, LANES), F32),
                        pltpu.VMEM((slabs, 2 * STATE_LANES // LANES, prow, LANES), F32),
                        pltpu.VMEM((slabs, prow, LANES), F32),
                        pltpu.VMEM((slabs, 2, SUBLANES, STATE_LANES), F32)],
        compiler_params=_params("arbitrary", "arbitrary", "arbitrary"),
        name="s5_ssm",
    )(u, bpack, cpack, coef, dvec.reshape(1, d))


def _ffn_up_kernel(h_ref, wu_ref, wv_ref, cwu_ref, cwv_ref, cbu_ref, cbv_ref, wd_ref, o_ref,
                   wdb_ref, wub_ref, wvb_ref, carry_ref, *, tiles_per_batch):
    i = pl.program_id(1)
    tm = h_ref.shape[0]
    wdb_ref[...] = wd_ref[...].astype(BF16)

    @pl.when(i == 0)
    def _():
        wub_ref[...] = wu_ref[...].astype(BF16)
        wvb_ref[...] = wv_ref[...].astype(BF16)

    first = (i % tiles_per_batch) == 0
    h = h_ref[...]
    conv = []
    for stream, (wb_ref, cw_ref, cb_ref) in enumerate(((wub_ref, cwu_ref, cbu_ref),
                                                        (wvb_ref, cwv_ref, cbv_ref))):
        a = jnp.dot(h, wb_ref[...], preferred_element_type=F32)
        pv = carry_ref[stream]
        prev = jnp.where(first, jnp.zeros_like(pv), pv)
        carry_ref[stream] = a[tm - SUBLANES:]
        ext = jnp.concatenate([prev, a], axis=0)
        cw = cw_ref[...]
        out = cw[CONV_WIDTH - 1:CONV_WIDTH] * a
        for lag in range(1, CONV_WIDTH):
            tap = CONV_WIDTH - 1 - lag
            out = out + cw[tap:tap + 1] * pltpu.roll(ext, lag, 0)[SUBLANES:]
        conv.append(out + cb_ref[...])
    cu, cv = conv
    o_ref[...] = (cu * jax.nn.sigmoid(cu) * cv).astype(o_ref.dtype)


def _ffn_up(h, w_up, conv_w, conv_b, w_down, layer, seq):
    n, d = h.shape
    ff = w_up.shape[2] // 2
    tm = _tile(seq, 1024, BF16_ROWS)
    tf = _tile(ff, 512, LANES)
    nf = ff // tf
    ntile = n // tm
    wrows = ff // (nf * ntile)
    assert wrows * nf * ntile == ff and wrows % BF16_ROWS == 0
    cb = conv_b.astype(F32).reshape(conv_b.shape[0], 1, 2 * ff)
    return pl.pallas_call(
        functools.partial(_ffn_up_kernel, tiles_per_batch=seq // tm),
        out_shape=[jax.ShapeDtypeStruct((n, ff), BF16),
                   jax.ShapeDtypeStruct((ff, w_down.shape[2]), BF16)],
        grid=(nf, ntile),
        in_specs=[pl.BlockSpec((tm, d), lambda f, i: (i, 0)),
                  pl.BlockSpec((None, d, tf), lambda f, i: (layer, 0, f)),
                  pl.BlockSpec((None, d, tf), lambda f, i: (layer, 0, nf + f)),
                  pl.BlockSpec((None, CONV_WIDTH, tf), lambda f, i: (layer, 0, f)),
                  pl.BlockSpec((None, CONV_WIDTH, tf), lambda f, i: (layer, 0, nf + f)),
                  pl.BlockSpec((None, 1, tf), lambda f, i: (layer, 0, f)),
                  pl.BlockSpec((None, 1, tf), lambda f, i: (layer, 0, nf + f)),
                  pl.BlockSpec((None, wrows, w_down.shape[2]), lambda f, i: (layer, f * ntile + i, 0))],
        out_specs=[pl.BlockSpec((tm, tf), lambda f, i: (i, f)),
                   pl.BlockSpec((wrows, w_down.shape[2]), lambda f, i: (f * ntile + i, 0))],
        scratch_shapes=[pltpu.VMEM((d, tf), BF16), pltpu.VMEM((d, tf), BF16),
                        pltpu.VMEM((2, SUBLANES, tf), F32)],
        compiler_params=_params("arbitrary", "arbitrary"),
        name="ffn_up_conv",
    )(h, w_up, w_up, conv_w.astype(F32), conv_w.astype(F32), cb, cb, w_down)


def _ffn_down_kernel(a_ref, w_ref, x_ref, gate_ref, *rest, kinds, emit_x, with_flog):
    na = _norm_arg_count(kinds)
    norm_refs = rest[:na]
    rest = rest[na:]
    if with_flog:
        wf_ref, rest = rest[0], rest[1:]
    outs = rest
    acc_ref = outs[0]
    k = pl.program_id(1)

    def contrib():
        return jnp.dot(a_ref[...], w_ref[...], preferred_element_type=F32)

    @pl.when(k == 0)
    def _():
        acc_ref[...] = contrib()

    @pl.when(k > 0)
    def _():
        acc_ref[...] += contrib()

    @pl.when(k == pl.num_programs(1) - 1)
    def _():
        xn = x_ref[...] + gate_ref[0] * acc_ref[...]
        q = 0
        if emit_x:
            outs[0][...] = xn
            q = 1
        first = _emit_norms(xn, kinds, norm_refs, outs[q:q + len(kinds)])
        if with_flog:
            hi = first.astype(BF16)
            lo = (first - hi.astype(F32)).astype(BF16)
            wf = wf_ref[...]
            whi = wf.astype(BF16)
            wlo = (wf - whi.astype(F32)).astype(BF16)
            nt = (((1,), (1,)), ((), ()))
            nh = wf.shape[0]
            both = lax.dot_general(jnp.concatenate([whi, wlo], axis=0), hi, nt,
                                   preferred_element_type=F32)
            outs[-1][...] = (both[:nh] + both[nh:]
                             + lax.dot_general(whi, lo, nt, preferred_element_type=F32))


def _ffn_down(act, x2, w_down, gate, kinds, norms, seq, *, emit_x, wf_t=None):
    n, ff = act.shape
    d = w_down.shape[1]
    tm = _tile(seq, 512, BF16_ROWS)
    tk = _tile(ff, 1536, LANES)
    tpb = seq // tm
    row = lambda i, k: (i, 0)
    bidx = lambda i, k: (i // tpb, 0, 0)
    nspecs, nargs = _norm_specs_args(kinds, norms, d, bidx)
    oshapes, ospecs = _norm_outs(kinds, n, d, tm, row)
    if emit_x:
        oshapes = [jax.ShapeDtypeStruct((n, d), F32)] + oshapes
        ospecs = [pl.BlockSpec((tm, d), row)] + ospecs
    assert oshapes[0].dtype == F32
    extra_specs, extra_args = [], []
    if wf_t is not None:
        nh = wf_t.shape[0]
        extra_specs = [pl.BlockSpec((nh, d), lambda i, k: (0, 0))]
        extra_args = [wf_t]
        oshapes = oshapes + [jax.ShapeDtypeStruct((nh, n), F32)]
        ospecs = ospecs + [pl.BlockSpec((nh, tm), lambda i, k: (0, i))]
    return pl.pallas_call(
        functools.partial(_ffn_down_kernel, kinds=tuple(kinds), emit_x=emit_x,
                          with_flog=wf_t is not None),
        out_shape=oshapes,
        grid=(n // tm, ff // tk),
        in_specs=[pl.BlockSpec((tm, tk), lambda i, k: (i, k)),
                  pl.BlockSpec((tk, d), lambda i, k: (k, 0)),
                  pl.BlockSpec((tm, d), row),
                  pl.BlockSpec((1, 1, d), bidx)] + nspecs + extra_specs,
        out_specs=ospecs,
        compiler_params=_params("arbitrary", "arbitrary"),
        name="ffn_down_res",
    )(act, w_down, x2, gate, *nargs, *extra_args)


def _fcum_kernel(z_ref, b_ref, tri_ref, o_ref):
    z = z_ref[...] + b_ref[...]
    lf = jnp.minimum(z, 0.0) - jnp.log1p(jnp.exp(-jnp.abs(z)))
    carry = jnp.zeros((z.shape[0], 1), F32)
    for cidx in range(z.shape[1] // LANES):
        sl = slice(cidx * LANES, (cidx + 1) * LANES)
        cs = jnp.dot(lf[:, sl], tri_ref[...], precision=lax.Precision.HIGHEST,
                     preferred_element_type=F32) + carry
        o_ref[:, sl] = cs * LOG2E
        carry = cs[:, LANES - 1:LANES]


def _fcum(flog_t, forget_b, bsz, seq):
    nh, n = flog_t.shape
    tri = (jnp.arange(LANES)[:, None] <= jnp.arange(LANES)[None, :]).astype(F32)
    return pl.pallas_call(
        _fcum_kernel,
        out_shape=jax.ShapeDtypeStruct((nh, n), F32),
        grid=(bsz,),
        in_specs=[pl.BlockSpec((nh, seq), lambda b: (0, b)),
                  pl.BlockSpec((nh, 1), lambda b: (0, 0)),
                  pl.BlockSpec((LANES, LANES), lambda b: (0, 0))],
        out_specs=pl.BlockSpec((nh, seq), lambda b: (0, b)),
        compiler_params=_params("arbitrary"),
        name="forget_cumsum",
    )(flog_t, forget_b.astype(F32).reshape(nh, 1), tri)


def _attn_kernel(q_ref, k_ref, v_ref, f_ref, mask_ref, o_ref, m_ref, l_ref, acc_ref, *, tk, heads):
    qi = pl.program_id(2)
    m_ref[...] = jnp.full_like(m_ref, -jnp.inf)
    l_ref[...] = jnp.zeros_like(l_ref)
    acc_ref[...] = jnp.zeros_like(acc_ref)
    ncol = tk // LANES

    def head_step(hd, cidx, masked):
        hs = slice(hd * HEAD_DIM, (hd + 1) * HEAD_DIM)
        k0 = pl.multiple_of(cidx * tk, tk)
        kc = k_ref[pl.ds(k0, tk), hs]
        vc = v_ref[pl.ds(k0, tk), hs]
        s = lax.dot_general(q_ref[:, hs], kc, (((1,), (1,)), ((), ())), preferred_element_type=F32)
        s = s - f_ref[hd, :, pl.ds(k0, tk)]
        if masked:
            s = s + mask_ref[...]
        cols_s = [s[:, j * LANES:(j + 1) * LANES] for j in range(ncol)]
        mc = cols_s[0]
        for sj in cols_s[1:]:
            mc = jnp.maximum(mc, sj)
        m_old = m_ref[hd]
        m_new = jnp.maximum(m_old, jnp.max(mc, axis=-1, keepdims=True))
        alpha = jnp.exp2(m_old - m_new)
        ps = [jnp.exp2(sj - m_new) for sj in cols_s]
        lsum = ps[0]
        for pj in ps[1:]:
            lsum = lsum + pj
        l_ref[hd] = alpha * l_ref[hd] + lsum
        p = jnp.concatenate(ps, axis=-1).astype(BF16)
        acc_ref[hd] = alpha * acc_ref[hd] + jnp.dot(p, vc, preferred_element_type=F32)
        m_ref[hd] = m_new

    def step(cidx, masked):
        for hd in range(heads):
            head_step(hd, cidx, masked)

    n_full = qi

    def pair_body(pidx, carry):
        step(2 * pidx, False)
        step(2 * pidx + 1, False)
        return carry

    lax.fori_loop(0, n_full // 2, pair_body, 0)

    @pl.when(n_full % 2 == 1)
    def _():
        step(n_full - 1, False)

    step(n_full, True)
    for hd in range(heads):
        l_row = jnp.sum(l_ref[hd], axis=-1, keepdims=True)
        o_ref[:, hd * HEAD_DIM:(hd + 1) * HEAD_DIM] = (acc_ref[hd] / l_row).astype(o_ref.dtype)


def _attention(q, kv, f_cum, bsz, seq):
    n, d = q.shape
    nh = d // HEAD_DIM
    heads = next(c for c in (ATTN_HEADS, 2, 1) if nh % c == 0)
    tq = _tile(seq, 512, LANES)
    tk = tq
    nq = seq // tq
    width = heads * HEAD_DIM
    ngrp = nh // heads
    causal = jnp.where(jnp.arange(tk)[None, :] <= jnp.arange(tq)[:, None], 0.0, -jnp.inf).astype(F32)
    return pl.pallas_call(
        functools.partial(_attn_kernel, tk=tk, heads=heads),
        out_shape=jax.ShapeDtypeStruct((n, d), BF16),
        grid=(bsz, ngrp, nq),
        in_specs=[pl.BlockSpec((tq, width), lambda b, h, i: (b * nq + i, h)),
                  pl.BlockSpec((seq, width), lambda b, h, i: (b, h)),
                  pl.BlockSpec((seq, width), lambda b, h, i: (b, ngrp + h)),
                  pl.BlockSpec((heads, 1, seq), lambda b, h, i: (h, 0, b)),
                  pl.BlockSpec((tq, tk), lambda b, h, i: (0, 0))],
        out_specs=pl.BlockSpec((tq, width), lambda b, h, i: (b * nq + i, h)),
        scratch_shapes=[pltpu.VMEM((heads, tq, LANES), F32), pltpu.VMEM((heads, tq, LANES), F32),
                        pltpu.VMEM((heads, tq, HEAD_DIM), F32)],
        compiler_params=_params("arbitrary", "arbitrary", "arbitrary"),
        name="fox_attention",
    )(q, kv, kv, f_cum.reshape(nh, 1, n), causal)


def kernel(x, c, mod_w, mod_b, norm_mix_g, norm_ffn_g, ssm_w_in, ssm_log_step, ssm_a_re, ssm_a_im,
           ssm_b_re, ssm_b_im, ssm_c_re, ssm_c_im, ssm_d, ssm_glu_w, ssm_glu_b, ssm_w_out,
           kv_mod_w, kv_mod_b, kv_norm_g, kv_w, forget_b, attn_w_q, attn_w_out,
           ffn_w_up, ffn_conv_w, ffn_conv_b, ffn_w_down, final_norm_g):
    bsz, seq, d = x.shape
    n = bsz * seq
    depth = mod_w.shape[0]
    n_a = ssm_w_in.shape[0]
    assert n_a >= 1 and d % LANES == 0
    x2 = x.astype(F32).reshape(n, d)

    c_rep = jnp.broadcast_to(c.astype(F32)[:, :, None], (bsz, d, LANES))

    def vecs(m, count):
        return [m[:, k * d:(k + 1) * d].reshape(bsz, 1, d) for k in range(count)]

    mods = [vecs(_gemv(c_rep, mod_w, i, mod_b[i]), N_MOD) for i in range(depth)]
    sh_kv, sc_kv = vecs(_gemv(c_rep, kv_mod_w.reshape(1, *kv_mod_w.shape), 0, kv_mod_b), 2)

    def mix_norm(i):
        return (norm_mix_g[i], mods[i][0], mods[i][1])

    def after_ffn(i):
        if i + 1 == depth:
            return ["plain"], [(final_norm_g,)]
        kinds, norms = [], []
        if i + 1 == n_a:
            kinds.append("mod")
            norms.append((kv_norm_g, sh_kv, sc_kv))
        kinds.append("mod")
        norms.append(mix_norm(i + 1))
        return kinds, norms

    h = _norm_mod(x2, *mix_norm(0), seq)
    kv = f_cum = None
    hk = flog_t = None
    out = None
    for i in range(depth):
        sh_m, sc_m, g_m, sh_f, sc_f, g_f = mods[i]
        ffn_norm = (norm_ffn_g[i], sh_f, sc_f)
        if i == n_a:
            kv = _mm(hk, kv_w, 0, 2 * d, name="kv_proj")
            f_cum = _fcum(flog_t, forget_b, bsz, seq)
        if i < n_a:
            u = _mm(h, ssm_w_in, i, d, name="ssm_in_proj")
            bpack, cpack, coef = _s5_pack(ssm_log_step[i], ssm_a_re[i], ssm_a_im[i], ssm_b_re[i],
                                          ssm_b_im[i], ssm_c_re[i], ssm_c_im[i], _ssm_chunk(seq)[1])
            g = _ssm(u, bpack, cpack, coef, ssm_d[i].astype(F32), bsz, seq)
            z = _mm(g, ssm_glu_w, i, d, glu_bias=ssm_glu_b[i].astype(F32), name="ssm_glu")
            x2, h = _mm_res(z, ssm_w_out, i, x2, g_m, ["mod"], [ffn_norm], seq, name="ssm_out_proj")
        else:
            j = i - n_a
            q = _mm(h, attn_w_q, j, d, out_scale=HEAD_DIM ** -0.5 * LOG2E, name="q_proj")
            o = _attention(q, kv, f_cum, bsz, seq)
            x2, h = _mm_res(o, attn_w_out, j, x2, g_m, ["mod"], [ffn_norm], seq, name="attn_out_proj")
        kinds, norms = after_ffn(i)
        last = i + 1 == depth
        want_kv = (i + 1 == n_a) and not last
        wf_t = kv_w[:, 2 * d:].astype(F32).T if want_kv else None
        act, w_down_bf = _ffn_up(h, ffn_w_up, ffn_conv_w, ffn_conv_b, ffn_w_down, i, seq)
        res = _ffn_down(act, x2, w_down_bf, g_f, kinds, norms, seq, emit_x=not last, wf_t=wf_t)
        if last:
            out = res[0]
        else:
            x2 = res[0]
            if want_kv:
                hk, h, flog_t = res[1], res[2], res[3]
            else:
                h = res[1]
    return out.reshape(bsz, seq, d).astype(x.dtype)
```

```python
import functools
import math

import jax
import jax.numpy as jnp
import numpy as np
from jax import lax
from jax.experimental import pallas as pl
from jax.experimental.pallas import tpu as pltpu

F32 = jnp.float32
BF16 = jnp.bfloat16

LANES = 128
SUBLANES = 8
BF16_ROWS = 16
VMEM_LIMIT_BYTES = 56 * 1024 * 1024

SSM_GROUP = 16
SSM_STATE = 64
HEAD_DIM = 128
CONV_WIDTH = 3
NORM_EPS = 1e-6
LOG2E = math.log2(math.e)
N_MOD = 6

GROUPS_PER_BLOCK = LANES // SSM_GROUP
STATE_LANES = GROUPS_PER_BLOCK * SSM_STATE
SCAN_STEPS = (1, 2, 4)
SSM_CHUNK = 512
SSM_SLABS = 8
ATTN_HEADS = 4
FFN_ROWS = 1024
FFN_ROW_BLOCKS = 1

def _params(*sem):
    return pltpu.CompilerParams(dimension_semantics=sem, vmem_limit_bytes=VMEM_LIMIT_BYTES)


def _tile(n, pref, align):
    if n <= pref:
        return n
    t = (pref // align) * align
    while t >= align:
        if n % t == 0:
            return t
        t -= align
    raise ValueError(f"no tile for {n}")


def _rms_hat(x):
    ms = jnp.mean(x * x, axis=-1, keepdims=True)
    return x * lax.rsqrt(ms + NORM_EPS)


def _modulate(y, shift, scale):
    return y * (1 + scale) + shift


def _gemv_kernel(c_ref, w_ref, b_ref, o_ref, cs_ref):
    @pl.when(pl.program_id(0) == 0)
    def _():
        cv = c_ref[...]
        cs_ref[...] = cv * jax.nn.sigmoid(cv)

    tn = w_ref.shape[1]
    for b in range(c_ref.shape[0]):
        cb = cs_ref[b]
        for lt in range(tn // LANES):
            sl = slice(lt * LANES, (lt + 1) * LANES)
            o_ref[b:b + 1, sl] = jnp.sum(w_ref[:, sl] * cb, axis=0, keepdims=True) + b_ref[:, sl]


def _gemv(c_rep, w_stack, layer, b):
    bsz, k, _ = c_rep.shape
    n = w_stack.shape[2]
    tn = _tile(n, 1024, LANES)
    return pl.pallas_call(
        _gemv_kernel,
        out_shape=jax.ShapeDtypeStruct((bsz, n), F32),
        grid=(n // tn,),
        in_specs=[pl.BlockSpec((bsz, k, LANES), lambda j: (0, 0, 0)),
                  pl.BlockSpec((None, k, tn), lambda j: (layer, 0, j)),
                  pl.BlockSpec((1, tn), lambda j: (0, j))],
        out_specs=pl.BlockSpec((bsz, tn), lambda j: (0, j)),
        scratch_shapes=[pltpu.VMEM((bsz, k, LANES), F32)],
        compiler_params=_params("arbitrary"),
        name="adaln_gemv",
    )(c_rep, w_stack, b.reshape(1, n))


def _norm_mod_kernel(x_ref, g_ref, sh_ref, sc_ref, o_ref):
    y = _rms_hat(x_ref[...]) * g_ref[...]
    o_ref[...] = _modulate(y, sh_ref[0], sc_ref[0]).astype(o_ref.dtype)


def _norm_mod(x2, g, shift, scale, seq):
    n, d = x2.shape
    tm = _tile(seq, 1024, BF16_ROWS)
    tpb = seq // tm
    vec = pl.BlockSpec((1, 1, d), lambda i: (i // tpb, 0, 0))
    return pl.pallas_call(
        _norm_mod_kernel,
        out_shape=jax.ShapeDtypeStruct((n, d), BF16),
        grid=(n // tm,),
        in_specs=[pl.BlockSpec((tm, d), lambda i: (i, 0)),
                  pl.BlockSpec((1, d), lambda i: (0, 0)), vec, vec],
        out_specs=pl.BlockSpec((tm, d), lambda i: (i, 0)),
        compiler_params=_params("arbitrary"),
        name="norm_mod",
    )(x2, g.reshape(1, d), shift, scale)


def _mm_kernel(a_ref, w_ref, *rest, glu, out_scale):
    *rest, wb_ref = rest

    @pl.when(pl.program_id(1) == 0)
    def _():
        wb_ref[...] = w_ref[...].astype(BF16)

    acc = jnp.dot(a_ref[...], wb_ref[...], preferred_element_type=F32)
    if glu:
        g_ref, b_ref, o_ref = rest
        o_ref[...] = (g_ref[...].astype(F32) * jax.nn.sigmoid(acc + b_ref[...])).astype(o_ref.dtype)
    else:
        (o_ref,) = rest
        if out_scale != 1.0:
            acc = acc * out_scale
        o_ref[...] = acc.astype(o_ref.dtype)


def _mm(a, w_stack, layer, nout, *, glu_bias=None, out_scale=1.0, name):
    n, k = a.shape
    tm = _tile(n, 1024, BF16_ROWS)
    tn = _tile(nout, 1024, LANES)
    if w_stack.ndim == 2:
        w_spec = pl.BlockSpec((k, tn), lambda j, i: (0, j))
    else:
        w_spec = pl.BlockSpec((None, k, tn), lambda j, i: (layer, 0, j))
    in_specs = [pl.BlockSpec((tm, k), lambda j, i: (i, 0)), w_spec]
    args = [a, w_stack]
    if glu_bias is not None:
        in_specs += [pl.BlockSpec((tm, tn), lambda j, i: (i, j)),
                     pl.BlockSpec((1, tn), lambda j, i: (0, j))]
        args += [a, glu_bias.reshape(1, nout)]
    return pl.pallas_call(
        functools.partial(_mm_kernel, glu=glu_bias is not None, out_scale=out_scale),
        out_shape=jax.ShapeDtypeStruct((n, nout), BF16),
        grid=(nout // tn, n // tm),
        in_specs=in_specs,
        out_specs=pl.BlockSpec((tm, tn), lambda j, i: (i, j)),
        scratch_shapes=[pltpu.VMEM((k, tn), BF16)],
        compiler_params=_params("arbitrary", "arbitrary"),
        name=name,
    )(*args)


def _norm_arg_count(kinds):
    return sum(3 if kd == "mod" else 1 for kd in kinds)


def _emit_norms(xn, kinds, norm_refs, out_refs):
    xhat = _rms_hat(xn)
    first = None
    p = 0
    for kd, o_ref in zip(kinds, out_refs):
        if kd == "mod":
            g_ref, sh_ref, sc_ref = norm_refs[p:p + 3]
            p += 3
            val = _modulate(xhat * g_ref[...], sh_ref[0], sc_ref[0])
        else:
            val = xhat * norm_refs[p][...]
            p += 1
        if first is None:
            first = val
        o_ref[...] = val.astype(o_ref.dtype)
    return first


def _norm_specs_args(kinds, norms, d, bidx):
    specs, args = [], []
    for kd, nm in zip(kinds, norms):
        specs.append(pl.BlockSpec((1, d), lambda *ix: (0, 0)))
        args.append(nm[0].reshape(1, d))
        if kd == "mod":
            for v in nm[1:]:
                specs.append(pl.BlockSpec((1, 1, d), bidx))
                args.append(v)
    return specs, args


def _norm_outs(kinds, n, d, tm, row_map):
    shapes = [jax.ShapeDtypeStruct((n, d), BF16 if kd == "mod" else F32) for kd in kinds]
    specs = [pl.BlockSpec((tm, d), row_map) for _ in kinds]
    return shapes, specs


def _mm_res_kernel(a_ref, w_ref, x_ref, gate_ref, *rest, kinds):
    *rest, wb_ref = rest
    na = _norm_arg_count(kinds)
    norm_refs, outs = rest[:na], rest[na:]

    @pl.when(pl.program_id(0) == 0)
    def _():
        wb_ref[...] = w_ref[...].astype(BF16)

    acc = jnp.dot(a_ref[...], wb_ref[...], preferred_element_type=F32)
    xn = x_ref[...] + gate_ref[0] * acc
    outs[0][...] = xn
    _emit_norms(xn, kinds, norm_refs, outs[1:])


def _mm_res(a, w_stack, layer, x2, gate, kinds, norms, seq, *, name):
    n, k = a.shape
    d = w_stack.shape[2]
    tm = _tile(seq, 512, BF16_ROWS)
    tpb = seq // tm
    bidx = lambda i: (i // tpb, 0, 0)
    row = lambda i: (i, 0)
    nspecs, nargs = _norm_specs_args(kinds, norms, d, bidx)
    oshapes, ospecs = _norm_outs(kinds, n, d, tm, row)
    return pl.pallas_call(
        functools.partial(_mm_res_kernel, kinds=tuple(kinds)),
        out_shape=[jax.ShapeDtypeStruct((n, d), F32)] + oshapes,
        grid=(n // tm,),
        in_specs=[pl.BlockSpec((tm, k), row),
                  pl.BlockSpec((None, k, d), lambda i: (layer, 0, 0), pipeline_mode=pl.Buffered(1)),
                  pl.BlockSpec((tm, d), row),
                  pl.BlockSpec((1, 1, d), bidx)] + nspecs,
        out_specs=[pl.BlockSpec((tm, d), row)] + ospecs,
        scratch_shapes=[pltpu.VMEM((k, d), BF16)],
        compiler_params=_params("arbitrary"),
        name=name,
    )(a, w_stack, x2, gate, *nargs)


def _ssm_kernel(u_ref, bp_ref, cp_ref, coef_ref, d_ref, o_ref, upad_ref, st_ref, ypad_ref,
                carry_ref, *, slabs, seg_len):
    s = STATE_LANES
    nseg = SUBLANES
    pitch = seg_len + 1
    nlt = s // LANES

    @pl.when(pl.program_id(2) == 0)
    def _():
        carry_ref[...] = jnp.zeros_like(carry_ref)

    def cmul(ar, ai, xr, xi):
        return ar * xr - ai * xi, ar * xi + ai * xr

    def load_rows(sb, half, rows):
        return jnp.concatenate([st_ref[sb, half * nlt + j, rows, :] for j in range(nlt)], axis=-1)

    def store_rows(sb, half, rows, val):
        for j in range(nlt):
            st_ref[sb, half * nlt + j, rows, :] = val[:, j * LANES:(j + 1) * LANES]

    for sb in range(slabs):
        ch = slice(sb * LANES, (sb + 1) * LANES)
        u32 = u_ref[:, ch].astype(F32)
        upad_ref[sb] = jnp.zeros(upad_ref.shape[1:], F32)
        for k in range(nseg):
            upad_ref[sb, pl.ds(k * pitch, seg_len), :] = u32[k * seg_len:(k + 1) * seg_len]
        bu = jnp.dot(upad_ref[sb].astype(BF16), bp_ref[sb], preferred_element_type=F32)
        for j in range(2 * nlt):
            st_ref[sb, j] = bu[:, j * LANES:(j + 1) * LANES]
        ar, ai = coef_ref[sb, 0], coef_ref[sb, 1]

        sr = jnp.zeros((nseg, s), F32)
        si = jnp.zeros((nseg, s), F32)
        for t in range(seg_len):
            rows = pl.ds(t, nseg, stride=pitch)
            pr_, pi_ = cmul(ar, ai, sr, si)
            sr = pr_ + load_rows(sb, 0, rows)
            si = pi_ + load_rows(sb, 1, rows)
            store_rows(sb, 0, rows, sr)
            store_rows(sb, 1, rows, si)

        tr, ti = sr, si
        for q, dlt in enumerate(SCAN_STEPS):
            hr, hi = coef_ref[sb, 2 + 2 * q], coef_ref[sb, 3 + 2 * q]
            dr, di = cmul(hr, hi, pltpu.roll(tr, dlt, 0), pltpu.roll(ti, dlt, 0))
            tr, ti = tr + dr, ti + di
        cr, ci = carry_ref[sb, 0], carry_ref[sb, 1]
        dr, di = cmul(coef_ref[sb, 8], coef_ref[sb, 9], cr, ci)
        tr, ti = tr + dr, ti + di
        row0 = lax.broadcasted_iota(jnp.int32, (nseg, s), 0) == 0
        zr = jnp.where(row0, cr, pltpu.roll(tr, 1, 0))
        zi = jnp.where(row0, ci, pltpu.roll(ti, 1, 0))
        carry_ref[sb, 0] = jnp.broadcast_to(tr[nseg - 1:nseg], tr.shape)
        carry_ref[sb, 1] = jnp.broadcast_to(ti[nseg - 1:nseg], ti.shape)

        for t in range(seg_len):
            rows = pl.ds(t, nseg, stride=pitch)
            zr, zi = cmul(ar, ai, zr, zi)
            store_rows(sb, 0, rows, load_rows(sb, 0, rows) + zr)
            store_rows(sb, 1, rows, load_rows(sb, 1, rows) + zi)

        states = jnp.concatenate([st_ref[sb, j] for j in range(2 * nlt)], axis=-1).astype(BF16)
        ypad_ref[sb] = jnp.dot(states, cp_ref[sb], preferred_element_type=F32)
        y = jnp.concatenate([ypad_ref[sb, pl.ds(k * pitch, seg_len), :] for k in range(nseg)], axis=0)
        y = y + d_ref[:, ch] * u32
        o_ref[:, ch] = jax.nn.gelu(y).astype(o_ref.dtype)


def _ssm_chunk(seq):
    tt = _tile(seq, SSM_CHUNK, SUBLANES * SUBLANES)
    return tt, tt // SUBLANES


def _s5_pack(log_step, a_re, a_im, b_re, b_im, c_re, c_im, seg_len):
    g, p = a_re.shape
    h = b_re.shape[-1]
    nblk = g // GROUPS_PER_BLOCK
    step = jnp.exp(log_step.astype(F32))[:, None]
    lam_re, lam_im = a_re.astype(F32), a_im.astype(F32)

    rows = np.arange(SUBLANES)
    expo = np.stack([np.ones(SUBLANES)] + [np.full(SUBLANES, seg_len * dlt) for dlt in SCAN_STEPS]
                    + [seg_len * (rows + 1.0)]).astype(np.float32)
    keep = np.stack([np.ones(SUBLANES)] + [rows >= dlt for dlt in SCAN_STEPS]
                    + [np.ones(SUBLANES)]).astype(np.float32)
    kk = expo[:, :, None, None]
    mag = jnp.exp(kk * (lam_re * step)) * keep[:, :, None, None]
    ang = kk * (lam_im * step)
    pow_re, pow_im = mag * jnp.cos(ang), mag * jnp.sin(ang)
    coef = jnp.stack([pow_re, pow_im], axis=1).reshape(len(expo), 2, SUBLANES, nblk, STATE_LANES)
    coef = coef.transpose(3, 0, 1, 2, 4).reshape(nblk, 2 * len(expo), SUBLANES, STATE_LANES)

    abar_re, abar_im = pow_re[0, 0], pow_im[0, 0]
    den = lam_re * lam_re + lam_im * lam_im
    nr = abar_re - 1.0
    fr = (nr * lam_re + abar_im * lam_im) / den
    fi = (abar_im * lam_re - nr * lam_im) / den
    br, bi = b_re.astype(F32), b_im.astype(F32)
    bbar_re = fr[..., None] * br - fi[..., None] * bi
    bbar_im = fr[..., None] * bi + fi[..., None] * br
    eye = jnp.eye(GROUPS_PER_BLOCK, dtype=F32)
    bb = jnp.stack([bbar_re, bbar_im]).reshape(2, nblk, GROUPS_PER_BLOCK, p, h)
    bpack = jnp.einsum("zjgph,gk->jghzkp", bb, eye).reshape(nblk, LANES, 2 * STATE_LANES)
    cc = jnp.stack([c_re.astype(F32), -c_im.astype(F32)]).reshape(2, nblk, GROUPS_PER_BLOCK, h, p)
    cpack = jnp.einsum("zjghp,gk->jzgpkh", cc, eye).reshape(nblk, 2 * STATE_LANES, LANES)
    return bpack.astype(BF16), cpack.astype(BF16), coef


def _ssm(u, bpack, cpack, coef, dvec, bsz, seq):
    n, d = u.shape
    nblk = d // LANES
    tt, seg_len = _ssm_chunk(seq)
    nt = seq // tt
    slabs = next(c for c in (SSM_SLABS, 2, 1) if nblk % c == 0)
    width = slabs * LANES
    prow =-(-(SUBLANES * (seg_len + 1)) // BF16_ROWS) * BF16_ROWS
    row = lambda b, j, t: (b * nt + t, j)
    return pl.pallas_call(
        functools.partial(_ssm_kernel, slabs=slabs, seg_len=seg_len),
        out_shape=jax.ShapeDtypeStruct((n, d), BF16),
        grid=(bsz, nblk // slabs, nt),
        in_specs=[pl.BlockSpec((tt, width), row),
                  pl.BlockSpec((slabs, LANES, 2 * STATE_LANES), lambda b, j, t: (j, 0, 0)),
                  pl.BlockSpec((slabs, 2 * STATE_LANES, LANES), lambda b, j, t: (j, 0, 0)),
                  pl.BlockSpec((slabs, 2 * len(SCAN_STEPS) + 4, SUBLANES, STATE_LANES),
                               lambda b, j, t: (j, 0, 0, 0)),
                  pl.BlockSpec((1, width), lambda b, j, t: (0, j))],
        out_specs=pl.BlockSpec((tt, width), row),
        scratch_shapes=[pltpu.VMEM((slabs, prow, LANES), F32),
                        pltpu.VMEM((slabs, 2 * STATE_LANES // LANES, prow, LANES), F32),
                        pltpu.VMEM((slabs, prow, LANES), F32),
                        pltpu.VMEM((slabs, 2, SUBLANES, STATE_LANES), F32)],
        compiler_params=_params("arbitrary", "arbitrary", "arbitrary"),
        name="s5_ssm",
    )(u, bpack, cpack, coef, dvec.reshape(1, d))


def _ffn_up_kernel(h_ref, wu_ref, wv_ref, cwu_ref, cwv_ref, cbu_ref, cbv_ref, wd_ref, o_ref,
                   wdb_ref, wub_ref, wvb_ref, carry_ref, *, tiles_per_batch, nsub):
    i = pl.program_id(1)
    tm = h_ref.shape[0]
    wdb_ref[...] = wd_ref[...].astype(BF16)

    @pl.when(i == 0)
    def _():
        wub_ref[...] = wu_ref[...].astype(BF16)
        wvb_ref[...] = wv_ref[...].astype(BF16)

    first = (i % tiles_per_batch) == 0
    prevs = []
    for stream in range(2):
        pv = carry_ref[stream]
        prevs.append(jnp.where(first, jnp.zeros_like(pv), pv))
    ts = tm // nsub
    for sb in range(nsub):
        rows = slice(sb * ts, (sb + 1) * ts)
        h = h_ref[rows, :]
        conv = []
        for stream, (wb_ref, cw_ref, cb_ref) in enumerate(((wub_ref, cwu_ref, cbu_ref),
                                                            (wvb_ref, cwv_ref, cbv_ref))):
            a = jnp.dot(h, wb_ref[...], preferred_element_type=F32)
            ext = jnp.concatenate([prevs[stream], a], axis=0)
            prevs[stream] = a[ts - SUBLANES:]
            cw = cw_ref[...]
            out = cw[CONV_WIDTH - 1:CONV_WIDTH] * a
            for lag in range(1, CONV_WIDTH):
                tap = CONV_WIDTH - 1 - lag
                out = out + cw[tap:tap + 1] * pltpu.roll(ext, lag, 0)[SUBLANES:]
            conv.append(out + cb_ref[...])
        cu, cv = conv
        o_ref[rows, :] = (cu * jax.nn.sigmoid(cu) * cv).astype(o_ref.dtype)
    for stream in range(2):
        carry_ref[stream] = prevs[stream]


def _ffn_up(h, w_up, conv_w, conv_b, w_down, layer, seq):
    n, d = h.shape
    ff = w_up.shape[2] // 2
    ts = _tile(seq, FFN_ROWS, BF16_ROWS)
    nsub = FFN_ROW_BLOCKS if seq % (FFN_ROW_BLOCKS * ts) == 0 else 1
    tm = nsub * ts
    tf = _tile(ff, 512, LANES)
    nf = ff // tf
    ntile = n // tm
    wrows = ff // (nf * ntile)
    assert wrows * nf * ntile == ff and wrows % BF16_ROWS == 0
    cb = conv_b.astype(F32).reshape(conv_b.shape[0], 1, 2 * ff)
    return pl.pallas_call(
        functools.partial(_ffn_up_kernel, tiles_per_batch=seq // tm, nsub=nsub),
        out_shape=[jax.ShapeDtypeStruct((n, ff), BF16),
                   jax.ShapeDtypeStruct((ff, w_down.shape[2]), BF16)],
        grid=(nf, ntile),
        in_specs=[pl.BlockSpec((tm, d), lambda f, i: (i, 0)),
                  pl.BlockSpec((None, d, tf), lambda f, i: (layer, 0, f)),
                  pl.BlockSpec((None, d, tf), lambda f, i: (layer, 0, nf + f)),
                  pl.BlockSpec((None, CONV_WIDTH, tf), lambda f, i: (layer, 0, f)),
                  pl.BlockSpec((None, CONV_WIDTH, tf), lambda f, i: (layer, 0, nf + f)),
                  pl.BlockSpec((None, 1, tf), lambda f, i: (layer, 0, f)),
                  pl.BlockSpec((None, 1, tf), lambda f, i: (layer, 0, nf + f)),
                  pl.BlockSpec((None, wrows, w_down.shape[2]), lambda f, i: (layer, f * ntile + i, 0))],
        out_specs=[pl.BlockSpec((tm, tf), lambda f, i: (i, f)),
                   pl.BlockSpec((wrows, w_down.shape[2]), lambda f, i: (f * ntile + i, 0))],
        scratch_shapes=[pltpu.VMEM((d, tf), BF16), pltpu.VMEM((d, tf), BF16),
                        pltpu.VMEM((2, SUBLANES, tf), F32)],
        compiler_params=_params("arbitrary", "arbitrary"),
        name="ffn_up_conv",
    )(h, w_up, w_up, conv_w.astype(F32), conv_w.astype(F32), cb, cb, w_down)


def _ffn_down_kernel(a_ref, w_ref, x_ref, gate_ref, *rest, kinds, emit_x, with_flog):
    na = _norm_arg_count(kinds)
    norm_refs = rest[:na]
    rest = rest[na:]
    if with_flog:
        wf_ref, rest = rest[0], rest[1:]
    outs = rest
    acc_ref = outs[0]
    k = pl.program_id(1)

    def contrib():
        return jnp.dot(a_ref[...], w_ref[...], preferred_element_type=F32)

    @pl.when(k == 0)
    def _():
        acc_ref[...] = contrib()

    @pl.when(k > 0)
    def _():
        acc_ref[...] += contrib()

    @pl.when(k == pl.num_programs(1) - 1)
    def _():
        xn = x_ref[...] + gate_ref[0] * acc_ref[...]
        q = 0
        if emit_x:
            outs[0][...] = xn
            q = 1
        first = _emit_norms(xn, kinds, norm_refs, outs[q:q + len(kinds)])
        if with_flog:
            hi = first.astype(BF16)
            lo = (first - hi.astype(F32)).astype(BF16)
            wf = wf_ref[...]
            whi = wf.astype(BF16)
            wlo = (wf - whi.astype(F32)).astype(BF16)
            nt = (((1,), (1,)), ((), ()))
            nh = wf.shape[0]
            both = lax.dot_general(jnp.concatenate([whi, wlo], axis=0), hi, nt,
                                   preferred_element_type=F32)
            outs[-1][...] = (both[:nh] + both[nh:]
                             + lax.dot_general(whi, lo, nt, preferred_element_type=F32))


def _ffn_down(act, x2, w_down, gate, kinds, norms, seq, *, emit_x, wf_t=None):
    n, ff = act.shape
    d = w_down.shape[1]
    tm = _tile(seq, 512, BF16_ROWS)
    tk = _tile(ff, 1536, LANES)
    tpb = seq // tm
    row = lambda i, k: (i, 0)
    bidx = lambda i, k: (i // tpb, 0, 0)
    nspecs, nargs = _norm_specs_args(kinds, norms, d, bidx)
    oshapes, ospecs = _norm_outs(kinds, n, d, tm, row)
    if emit_x:
        oshapes = [jax.ShapeDtypeStruct((n, d), F32)] + oshapes
        ospecs = [pl.BlockSpec((tm, d), row)] + ospecs
    assert oshapes[0].dtype == F32
    extra_specs, extra_args = [], []
    if wf_t is not None:
        nh = wf_t.shape[0]
        extra_specs = [pl.BlockSpec((nh, d), lambda i, k: (0, 0))]
        extra_args = [wf_t]
        oshapes = oshapes + [jax.ShapeDtypeStruct((nh, n), F32)]
        ospecs = ospecs + [pl.BlockSpec((nh, tm), lambda i, k: (0, i))]
    return pl.pallas_call(
        functools.partial(_ffn_down_kernel, kinds=tuple(kinds), emit_x=emit_x,
                          with_flog=wf_t is not None),
        out_shape=oshapes,
        grid=(n // tm, ff // tk),
        in_specs=[pl.BlockSpec((tm, tk), lambda i, k: (i, k)),
                  pl.BlockSpec((tk, d), lambda i, k: (k, 0)),
                  pl.BlockSpec((tm, d), row),
                  pl.BlockSpec((1, 1, d), bidx)] + nspecs + extra_specs,
        out_specs=ospecs,
        compiler_params=_params("arbitrary", "arbitrary"),
        name="ffn_down_res",
    )(act, w_down, x2, gate, *nargs, *extra_args)


def _fcum_kernel(z_ref, b_ref, tri_ref, o_ref):
    z = z_ref[...] + b_ref[...]
    lf = jnp.minimum(z, 0.0) - jnp.log1p(jnp.exp(-jnp.abs(z)))
    carry = jnp.zeros((z.shape[0], 1), F32)
    for cidx in range(z.shape[1] // LANES):
        sl = slice(cidx * LANES, (cidx + 1) * LANES)
        cs = jnp.dot(lf[:, sl], tri_ref[...], precision=lax.Precision.HIGHEST,
                     preferred_element_type=F32) + carry
        o_ref[:, sl] = cs * LOG2E
        carry = cs[:, LANES - 1:LANES]


def _fcum(flog_t, forget_b, bsz, seq):
    nh, n = flog_t.shape
    tri = (jnp.arange(LANES)[:, None] <= jnp.arange(LANES)[None, :]).astype(F32)
    return pl.pallas_call(
        _fcum_kernel,
        out_shape=jax.ShapeDtypeStruct((nh, n), F32),
        grid=(bsz,),
        in_specs=[pl.BlockSpec((nh, seq), lambda b: (0, b)),
                  pl.BlockSpec((nh, 1), lambda b: (0, 0)),
                  pl.BlockSpec((LANES, LANES), lambda b: (0, 0))],
        out_specs=pl.BlockSpec((nh, seq), lambda b: (0, b)),
        compiler_params=_params("arbitrary"),
        name="forget_cumsum",
    )(flog_t, forget_b.astype(F32).reshape(nh, 1), tri)


def _attn_kernel(q_ref, k_ref, v_ref, f_ref, mask_ref, o_ref, m_ref, l_ref, acc_ref, *, tk, heads):
    qi = pl.program_id(2)
    m_ref[...] = jnp.full_like(m_ref, -jnp.inf)
    l_ref[...] = jnp.zeros_like(l_ref)
    acc_ref[...] = jnp.zeros_like(acc_ref)
    ncol = tk // LANES

    def head_step(hd, cidx, masked):
        hs = slice(hd * HEAD_DIM, (hd + 1) * HEAD_DIM)
        k0 = pl.multiple_of(cidx * tk, tk)
        kc = k_ref[pl.ds(k0, tk), hs]
        vc = v_ref[pl.ds(k0, tk), hs]
        s = lax.dot_general(q_ref[:, hs], kc, (((1,), (1,)), ((), ())), preferred_element_type=F32)
        s = s - f_ref[hd, :, pl.ds(k0, tk)]
        if masked:
            s = s + mask_ref[...]
        cols_s = [s[:, j * LANES:(j + 1) * LANES] for j in range(ncol)]
        mc = cols_s[0]
        for sj in cols_s[1:]:
            mc = jnp.maximum(mc, sj)
        m_old = m_ref[hd]
        m_new = jnp.maximum(m_old, jnp.max(mc, axis=-1, keepdims=True))
        alpha = jnp.exp2(m_old - m_new)
        ps = [jnp.exp2(sj - m_new) for sj in cols_s]
        lsum = ps[0]
        for pj in ps[1:]:
            lsum = lsum + pj
        l_ref[hd] = alpha * l_ref[hd] + lsum
        p = jnp.concatenate(ps, axis=-1).astype(BF16)
        acc_ref[hd] = alpha * acc_ref[hd] + jnp.dot(p, vc, preferred_element_type=F32)
        m_ref[hd] = m_new

    def step(cidx, masked):
        for hd in range(heads):
            head_step(hd, cidx, masked)

    n_full = qi

    def pair_body(pidx, carry):
        step(2 * pidx, False)
        step(2 * pidx + 1, False)
        return carry

    lax.fori_loop(0, n_full // 2, pair_body, 0)

    @pl.when(n_full % 2 == 1)
    def _():
        step(n_full - 1, False)

    step(n_full, True)
    for hd in range(heads):
        l_row = jnp.sum(l_ref[hd], axis=-1, keepdims=True)
        o_ref[:, hd * HEAD_DIM:(hd + 1) * HEAD_DIM] = (acc_ref[hd] / l_row).astype(o_ref.dtype)


def _attention(q, kv, f_cum, bsz, seq):
    n, d = q.shape
    nh = d // HEAD_DIM
    heads = next(c for c in (ATTN_HEADS, 2, 1) if nh % c == 0)
    tq = _tile(seq, 512, LANES)
    tk = tq
    nq = seq // tq
    width = heads * HEAD_DIM
    ngrp = nh // heads
    causal = jnp.where(jnp.arange(tk)[None, :] <= jnp.arange(tq)[:, None], 0.0, -jnp.inf).astype(F32)
    return pl.pallas_call(
        functools.partial(_attn_kernel, tk=tk, heads=heads),
        out_shape=jax.ShapeDtypeStruct((n, d), BF16),
        grid=(bsz, ngrp, nq),
        in_specs=[pl.BlockSpec((tq, width), lambda b, h, i: (b * nq + i, h)),
                  pl.BlockSpec((seq, width), lambda b, h, i: (b, h)),
                  pl.BlockSpec((seq, width), lambda b, h, i: (b, ngrp + h)),
                  pl.BlockSpec((heads, 1, seq), lambda b, h, i: (h, 0, b)),
                  pl.BlockSpec((tq, tk), lambda b, h, i: (0, 0))],
        out_specs=pl.BlockSpec((tq, width), lambda b, h, i: (b * nq + i, h)),
        scratch_shapes=[pltpu.VMEM((heads, tq, LANES), F32), pltpu.VMEM((heads, tq, LANES), F32),
                        pltpu.VMEM((heads, tq, HEAD_DIM), F32)],
        compiler_params=_params("arbitrary", "arbitrary", "arbitrary"),
        name="fox_attention",
    )(q, kv, kv, f_cum.reshape(nh, 1, n), causal)


def kernel(x, c, mod_w, mod_b, norm_mix_g, norm_ffn_g, ssm_w_in, ssm_log_step, ssm_a_re, ssm_a_im,
           ssm_b_re, ssm_b_im, ssm_c_re, ssm_c_im, ssm_d, ssm_glu_w, ssm_glu_b, ssm_w_out,
           kv_mod_w, kv_mod_b, kv_norm_g, kv_w, forget_b, attn_w_q, attn_w_out,
           ffn_w_up, ffn_conv_w, ffn_conv_b, ffn_w_down, final_norm_g):
    bsz, seq, d = x.shape
    n = bsz * seq
    depth = mod_w.shape[0]
    n_a = ssm_w_in.shape[0]
    assert n_a >= 1 and d % LANES == 0
    x2 = x.astype(F32).reshape(n, d)

    c_rep = jnp.broadcast_to(c.astype(F32)[:, :, None], (bsz, d, LANES))

    def vecs(m, count):
        return [m[:, k * d:(k + 1) * d].reshape(bsz, 1, d) for k in range(count)]

    mods = [vecs(_gemv(c_rep, mod_w, i, mod_b[i]), N_MOD) for i in range(depth)]
    sh_kv, sc_kv = vecs(_gemv(c_rep, kv_mod_w.reshape(1, *kv_mod_w.shape), 0, kv_mod_b), 2)

    def mix_norm(i):
        return (norm_mix_g[i], mods[i][0], mods[i][1])

    def after_ffn(i):
        if i + 1 == depth:
            return ["plain"], [(final_norm_g,)]
        kinds, norms = [], []
        if i + 1 == n_a:
            kinds.append("mod")
            norms.append((kv_norm_g, sh_kv, sc_kv))
        kinds.append("mod")
        norms.append(mix_norm(i + 1))
        return kinds, norms

    h = _norm_mod(x2, *mix_norm(0), seq)
    kv = f_cum = None
    hk = flog_t = None
    out = None
    for i in range(depth):
        sh_m, sc_m, g_m, sh_f, sc_f, g_f = mods[i]
        ffn_norm = (norm_ffn_g[i], sh_f, sc_f)
        if i == n_a:
            kv = _mm(hk, kv_w, 0, 2 * d, name="kv_proj")
            f_cum = _fcum(flog_t, forget_b, bsz, seq)
        if i < n_a:
            u = _mm(h, ssm_w_in, i, d, name="ssm_in_proj")
            bpack, cpack, coef = _s5_pack(ssm_log_step[i], ssm_a_re[i], ssm_a_im[i], ssm_b_re[i],
                                          ssm_b_im[i], ssm_c_re[i], ssm_c_im[i], _ssm_chunk(seq)[1])
            g = _ssm(u, bpack, cpack, coef, ssm_d[i].astype(F32), bsz, seq)
            z = _mm(g, ssm_glu_w, i, d, glu_bias=ssm_glu_b[i].astype(F32), name="ssm_glu")
            x2, h = _mm_res(z, ssm_w_out, i, x2, g_m, ["mod"], [ffn_norm], seq, name="ssm_out_proj")
        else:
            j = i - n_a
            q = _mm(h, attn_w_q, j, d, out_scale=HEAD_DIM ** -0.5 * LOG2E, name="q_proj")
            o = _attention(q, kv, f_cum, bsz, seq)
            x2, h = _mm_res(o, attn_w_out, j, x2, g_m, ["mod"], [ffn_norm], seq, name="attn_out_proj")
        kinds, norms = after_ffn(i)
        last = i + 1 == depth
        want_kv = (i + 1 == n_a) and not last
        wf_t = kv_w[:, 2 * d:].astype(F32).T if want_kv else None
        act, w_down_bf = _ffn_up(h, ffn_w_up, ffn_conv_w, ffn_conv_b, ffn_w_down, i, seq)
        res = _ffn_down(act, x2, w_down_bf, g_f, kinds, norms, seq, emit_x=not last, wf_t=wf_t)
        if last:
            out = res[0]
        else:
            x2 = res[0]
            if want_kv:
                hk, h, flog_t = res[1], res[2], res[3]
            else:
                h = res[1]
    return out.reshape(bsz, seq, d).astype(x.dtype)
```

```python
import functools
import math

import jax
import jax.numpy as jnp
import numpy as np
from jax import lax
from jax.experimental import pallas as pl
from jax.experimental.pallas import tpu as pltpu

F32 = jnp.float32
BF16 = jnp.bfloat16

LANES = 128
SUBLANES = 8
BF16_ROWS = 16
VMEM_LIMIT_BYTES = 56 * 1024 * 1024

SSM_GROUP = 16
SSM_STATE = 64
HEAD_DIM = 128
CONV_WIDTH = 3
NORM_EPS = 1e-6
LOG2E = math.log2(math.e)
N_MOD = 6

GROUPS_PER_BLOCK = LANES // SSM_GROUP
STATE_LANES = GROUPS_PER_BLOCK * SSM_STATE
SCAN_STEPS = (1, 2, 4)
SSM_CHUNK = 512
SSM_SLABS = 8
ATTN_HEADS = 4
FFN_ROWS = 1024
FFN_ROW_BLOCKS = 1
ROW_TILE = 512
MM_TILE = 1024
HIDDEN_BLOCK = 512
FFN_DOWN_K = 1536


def _params(*sem):
    return pltpu.CompilerParams(dimension_semantics=sem, vmem_limit_bytes=VMEM_LIMIT_BYTES)


def _tile(n, pref, align):
    if n <= pref:
        return n
    t = (pref // align) * align
    while t >= align:
        if n % t == 0:
            return t
        t -= align
    raise ValueError(f"no tile for {n}")


def _rms_hat(x):
    ms = jnp.mean(x * x, axis=-1, keepdims=True)
    return x * lax.rsqrt(ms + NORM_EPS)


def _modulate(y, shift, scale):
    return y * (1 + scale) + shift


def _gemv_kernel(c_ref, w_ref, b_ref, o_ref, cs_ref):
    @pl.when(pl.program_id(0) == 0)
    def _():
        cv = c_ref[...]
        cs_ref[...] = cv * jax.nn.sigmoid(cv)

    tn = w_ref.shape[1]
    for b in range(c_ref.shape[0]):
        cb = cs_ref[b]
        for lt in range(tn // LANES):
            sl = slice(lt * LANES, (lt + 1) * LANES)
            o_ref[b:b + 1, sl] = jnp.sum(w_ref[:, sl] * cb, axis=0, keepdims=True) + b_ref[:, sl]


def _gemv(c_rep, w_stack, layer, b):
    bsz, k, _ = c_rep.shape
    n = w_stack.shape[2]
    tn = _tile(n, MM_TILE, LANES)
    return pl.pallas_call(
        _gemv_kernel,
        out_shape=jax.ShapeDtypeStruct((bsz, n), F32),
        grid=(n // tn,),
        in_specs=[pl.BlockSpec((bsz, k, LANES), lambda j: (0, 0, 0)),
                  pl.BlockSpec((None, k, tn), lambda j: (layer, 0, j)),
                  pl.BlockSpec((1, tn), lambda j: (0, j))],
        out_specs=pl.BlockSpec((bsz, tn), lambda j: (0, j)),
        scratch_shapes=[pltpu.VMEM((bsz, k, LANES), F32)],
        compiler_params=_params("arbitrary"),
        name="adaln_gemv",
    )(c_rep, w_stack, b.reshape(1, n))


def _norm_mod_kernel(x_ref, g_ref, sh_ref, sc_ref, o_ref):
    y = _rms_hat(x_ref[...]) * g_ref[...]
    o_ref[...] = _modulate(y, sh_ref[0], sc_ref[0]).astype(o_ref.dtype)


def _norm_mod(x2, g, shift, scale, seq):
    n, d = x2.shape
    tm = _tile(seq, MM_TILE, BF16_ROWS)
    tpb = seq // tm
    vec = pl.BlockSpec((1, 1, d), lambda i: (i // tpb, 0, 0))
    return pl.pallas_call(
        _norm_mod_kernel,
        out_shape=jax.ShapeDtypeStruct((n, d), BF16),
        grid=(n // tm,),
        in_specs=[pl.BlockSpec((tm, d), lambda i: (i, 0)),
                  pl.BlockSpec((1, d), lambda i: (0, 0)), vec, vec],
        out_specs=pl.BlockSpec((tm, d), lambda i: (i, 0)),
        compiler_params=_params("arbitrary"),
        name="norm_mod",
    )(x2, g.reshape(1, d), shift, scale)


def _mm_kernel(a_ref, w_ref, *rest, glu, out_scale):
    *rest, wb_ref = rest

    @pl.when(pl.program_id(1) == 0)
    def _():
        wb_ref[...] = w_ref[...].astype(BF16)

    acc = jnp.dot(a_ref[...], wb_ref[...], preferred_element_type=F32)
    if glu:
        g_ref, b_ref, o_ref = rest
        o_ref[...] = (g_ref[...].astype(F32) * jax.nn.sigmoid(acc + b_ref[...])).astype(o_ref.dtype)
    else:
        (o_ref,) = rest
        if out_scale != 1.0:
            acc = acc * out_scale
        o_ref[...] = acc.astype(o_ref.dtype)


def _mm(a, w_stack, layer, nout, *, glu_bias=None, out_scale=1.0, name):
    n, k = a.shape
    tm = _tile(n, MM_TILE, BF16_ROWS)
    tn = _tile(nout, MM_TILE, LANES)
    if w_stack.ndim == 2:
        w_spec = pl.BlockSpec((k, tn), lambda j, i: (0, j))
    else:
        w_spec = pl.BlockSpec((None, k, tn), lambda j, i: (layer, 0, j))
    in_specs = [pl.BlockSpec((tm, k), lambda j, i: (i, 0)), w_spec]
    args = [a, w_stack]
    if glu_bias is not None:
        in_specs += [pl.BlockSpec((tm, tn), lambda j, i: (i, j)),
                     pl.BlockSpec((1, tn), lambda j, i: (0, j))]
        args += [a, glu_bias.reshape(1, nout)]
    return pl.pallas_call(
        functools.partial(_mm_kernel, glu=glu_bias is not None, out_scale=out_scale),
        out_shape=jax.ShapeDtypeStruct((n, nout), BF16),
        grid=(nout // tn, n // tm),
        in_specs=in_specs,
        out_specs=pl.BlockSpec((tm, tn), lambda j, i: (i, j)),
        scratch_shapes=[pltpu.VMEM((k, tn), BF16)],
        compiler_params=_params("arbitrary", "arbitrary"),
        name=name,
    )(*args)


def _norm_arg_count(kinds):
    return sum(3 if kd == "mod" else 1 for kd in kinds)


def _emit_norms(xn, kinds, norm_refs, out_refs):
    xhat = _rms_hat(xn)
    first = None
    p = 0
    for kd, o_ref in zip(kinds, out_refs):
        if kd == "mod":
            g_ref, sh_ref, sc_ref = norm_refs[p:p + 3]
            p += 3
            val = _modulate(xhat * g_ref[...], sh_ref[0], sc_ref[0])
        else:
            val = xhat * norm_refs[p][...]
            p += 1
        if first is None:
            first = val
        o_ref[...] = val.astype(o_ref.dtype)
    return first


def _norm_specs_args(kinds, norms, d, bidx):
    specs, args = [], []
    for kd, nm in zip(kinds, norms):
        specs.append(pl.BlockSpec((1, d), lambda *ix: (0, 0)))
        args.append(nm[0].reshape(1, d))
        if kd == "mod":
            for v in nm[1:]:
                specs.append(pl.BlockSpec((1, 1, d), bidx))
                args.append(v)
    return specs, args


def _norm_outs(kinds, n, d, tm, row_map):
    shapes = [jax.ShapeDtypeStruct((n, d), BF16 if kd == "mod" else F32) for kd in kinds]
    specs = [pl.BlockSpec((tm, d), row_map) for _ in kinds]
    return shapes, specs


def _mm_res_kernel(a_ref, w_ref, x_ref, gate_ref, *rest, kinds):
    *rest, wb_ref = rest
    na = _norm_arg_count(kinds)
    norm_refs, outs = rest[:na], rest[na:]

    @pl.when(pl.program_id(0) == 0)
    def _():
        wb_ref[...] = w_ref[...].astype(BF16)

    acc = jnp.dot(a_ref[...], wb_ref[...], preferred_element_type=F32)
    xn = x_ref[...] + gate_ref[0] * acc
    outs[0][...] = xn
    _emit_norms(xn, kinds, norm_refs, outs[1:])


def _mm_res(a, w_stack, layer, x2, gate, kinds, norms, seq, *, name):
    n, k = a.shape
    d = w_stack.shape[2]
    tm = _tile(seq, ROW_TILE, BF16_ROWS)
    tpb = seq // tm
    bidx = lambda i: (i // tpb, 0, 0)
    row = lambda i: (i, 0)
    nspecs, nargs = _norm_specs_args(kinds, norms, d, bidx)
    oshapes, ospecs = _norm_outs(kinds, n, d, tm, row)
    return pl.pallas_call(
        functools.partial(_mm_res_kernel, kinds=tuple(kinds)),
        out_shape=[jax.ShapeDtypeStruct((n, d), F32)] + oshapes,
        grid=(n // tm,),
        in_specs=[pl.BlockSpec((tm, k), row),
                  pl.BlockSpec((None, k, d), lambda i: (layer, 0, 0), pipeline_mode=pl.Buffered(1)),
                  pl.BlockSpec((tm, d), row),
                  pl.BlockSpec((1, 1, d), bidx)] + nspecs,
        out_specs=[pl.BlockSpec((tm, d), row)] + ospecs,
        scratch_shapes=[pltpu.VMEM((k, d), BF16)],
        compiler_params=_params("arbitrary"),
        name=name,
    )(a, w_stack, x2, gate, *nargs)


def _ssm_kernel(u_ref, bp_ref, cp_ref, coef_ref, d_ref, o_ref, upad_ref, st_ref, ypad_ref,
                carry_ref, *, slabs, seg_len):
    s = STATE_LANES
    nseg = SUBLANES
    pitch = seg_len + 1
    nlt = s // LANES

    @pl.when(pl.program_id(2) == 0)
    def _():
        carry_ref[...] = jnp.zeros_like(carry_ref)

    def cmul(ar, ai, xr, xi):
        return ar * xr - ai * xi, ar * xi + ai * xr

    def load_rows(sb, half, rows):
        return jnp.concatenate([st_ref[sb, half * nlt + j, rows, :] for j in range(nlt)], axis=-1)

    def store_rows(sb, half, rows, val):
        for j in range(nlt):
            st_ref[sb, half * nlt + j, rows, :] = val[:, j * LANES:(j + 1) * LANES]

    for sb in range(slabs):
        ch = slice(sb * LANES, (sb + 1) * LANES)
        u32 = u_ref[:, ch].astype(F32)
        upad_ref[sb] = jnp.zeros(upad_ref.shape[1:], F32)
        for k in range(nseg):
            upad_ref[sb, pl.ds(k * pitch, seg_len), :] = u32[k * seg_len:(k + 1) * seg_len]
        bu = jnp.dot(upad_ref[sb].astype(BF16), bp_ref[sb], preferred_element_type=F32)
        for j in range(2 * nlt):
            st_ref[sb, j] = bu[:, j * LANES:(j + 1) * LANES]
        ar, ai = coef_ref[sb, 0], coef_ref[sb, 1]

        sr = jnp.zeros((nseg, s), F32)
        si = jnp.zeros((nseg, s), F32)
        for t in range(seg_len):
            rows = pl.ds(t, nseg, stride=pitch)
            pr_, pi_ = cmul(ar, ai, sr, si)
            sr = pr_ + load_rows(sb, 0, rows)
            si = pi_ + load_rows(sb, 1, rows)
            store_rows(sb, 0, rows, sr)
            store_rows(sb, 1, rows, si)

        tr, ti = sr, si
        for q, dlt in enumerate(SCAN_STEPS):
            hr, hi = coef_ref[sb, 2 + 2 * q], coef_ref[sb, 3 + 2 * q]
            dr, di = cmul(hr, hi, pltpu.roll(tr, dlt, 0), pltpu.roll(ti, dlt, 0))
            tr, ti = tr + dr, ti + di
        cr, ci = carry_ref[sb, 0], carry_ref[sb, 1]
        dr, di = cmul(coef_ref[sb, 8], coef_ref[sb, 9], cr, ci)
        tr, ti = tr + dr, ti + di
        row0 = lax.broadcasted_iota(jnp.int32, (nseg, s), 0) == 0
        zr = jnp.where(row0, cr, pltpu.roll(tr, 1, 0))
        zi = jnp.where(row0, ci, pltpu.roll(ti, 1, 0))
        carry_ref[sb, 0] = jnp.broadcast_to(tr[nseg - 1:nseg], tr.shape)
        carry_ref[sb, 1] = jnp.broadcast_to(ti[nseg - 1:nseg], ti.shape)

        for t in range(seg_len):
            rows = pl.ds(t, nseg, stride=pitch)
            zr, zi = cmul(ar, ai, zr, zi)
            store_rows(sb, 0, rows, load_rows(sb, 0, rows) + zr)
            store_rows(sb, 1, rows, load_rows(sb, 1, rows) + zi)

        states = jnp.concatenate([st_ref[sb, j] for j in range(2 * nlt)], axis=-1).astype(BF16)
        ypad_ref[sb] = jnp.dot(states, cp_ref[sb], preferred_element_type=F32)
        y = jnp.concatenate([ypad_ref[sb, pl.ds(k * pitch, seg_len), :] for k in range(nseg)], axis=0)
        y = y + d_ref[:, ch] * u32
        o_ref[:, ch] = jax.nn.gelu(y).astype(o_ref.dtype)


def _ssm_chunk(seq):
    tt = _tile(seq, SSM_CHUNK, SUBLANES * SUBLANES)
    return tt, tt // SUBLANES


def _s5_pack(log_step, a_re, a_im, b_re, b_im, c_re, c_im, seg_len):
    g, p = a_re.shape
    h = b_re.shape[-1]
    nblk = g // GROUPS_PER_BLOCK
    step = jnp.exp(log_step.astype(F32))[:, None]
    lam_re, lam_im = a_re.astype(F32), a_im.astype(F32)

    rows = np.arange(SUBLANES)
    expo = np.stack([np.ones(SUBLANES)] + [np.full(SUBLANES, seg_len * dlt) for dlt in SCAN_STEPS]
                    + [seg_len * (rows + 1.0)]).astype(np.float32)
    keep = np.stack([np.ones(SUBLANES)] + [rows >= dlt for dlt in SCAN_STEPS]
                    + [np.ones(SUBLANES)]).astype(np.float32)
    kk = expo[:, :, None, None]
    mag = jnp.exp(kk * (lam_re * step)) * keep[:, :, None, None]
    ang = kk * (lam_im * step)
    pow_re, pow_im = mag * jnp.cos(ang), mag * jnp.sin(ang)
    coef = jnp.stack([pow_re, pow_im], axis=1).reshape(len(expo), 2, SUBLANES, nblk, STATE_LANES)
    coef = coef.transpose(3, 0, 1, 2, 4).reshape(nblk, 2 * len(expo), SUBLANES, STATE_LANES)

    abar_re, abar_im = pow_re[0, 0], pow_im[0, 0]
    den = lam_re * lam_re + lam_im * lam_im
    nr = abar_re - 1.0
    fr = (nr * lam_re + abar_im * lam_im) / den
    fi = (abar_im * lam_re - nr * lam_im) / den
    br, bi = b_re.astype(F32), b_im.astype(F32)
    bbar_re = fr[..., None] * br - fi[..., None] * bi
    bbar_im = fr[..., None] * bi + fi[..., None] * br
    eye = jnp.eye(GROUPS_PER_BLOCK, dtype=F32)
    bb = jnp.stack([bbar_re, bbar_im]).reshape(2, nblk, GROUPS_PER_BLOCK, p, h)
    bpack = jnp.einsum("zjgph,gk->jghzkp", bb, eye).reshape(nblk, LANES, 2 * STATE_LANES)
    cc = jnp.stack([c_re.astype(F32), -c_im.astype(F32)]).reshape(2, nblk, GROUPS_PER_BLOCK, h, p)
    cpack = jnp.einsum("zjghp,gk->jzgpkh", cc, eye).reshape(nblk, 2 * STATE_LANES, LANES)
    return bpack.astype(BF16), cpack.astype(BF16), coef


def _ssm(u, bpack, cpack, coef, dvec, bsz, seq):
    n, d = u.shape
    nblk = d // LANES
    tt, seg_len = _ssm_chunk(seq)
    nt = seq // tt
    slabs = next(c for c in (SSM_SLABS, 2, 1) if nblk % c == 0)
    width = slabs * LANES
    prow = -(-(SUBLANES * (seg_len + 1)) // BF16_ROWS) * BF16_ROWS
    row = lambda b, j, t: (b * nt + t, j)
    return pl.pallas_call(
        functools.partial(_ssm_kernel, slabs=slabs, seg_len=seg_len),
        out_shape=jax.ShapeDtypeStruct((n, d), BF16),
        grid=(bsz, nblk // slabs, nt),
        in_specs=[pl.BlockSpec((tt, width), row),
                  pl.BlockSpec((slabs, LANES, 2 * STATE_LANES), lambda b, j, t: (j, 0, 0)),
                  pl.BlockSpec((slabs, 2 * STATE_LANES, LANES), lambda b, j, t: (j, 0, 0)),
                  pl.BlockSpec((slabs, 2 * len(SCAN_STEPS) + 4, SUBLANES, STATE_LANES),
                               lambda b, j, t: (j, 0, 0, 0)),
                  pl.BlockSpec((1, width), lambda b, j, t: (0, j))],
        out_specs=pl.BlockSpec((tt, width), row),
        scratch_shapes=[pltpu.VMEM((slabs, prow, LANES), F32),
                        pltpu.VMEM((slabs, 2 * STATE_LANES // LANES, prow, LANES), F32),
                        pltpu.VMEM((slabs, prow, LANES), F32),
                        pltpu.VMEM((slabs, 2, SUBLANES, STATE_LANES), F32)],
        compiler_params=_params("arbitrary", "arbitrary", "arbitrary"),
        name="s5_ssm",
    )(u, bpack, cpack, coef, dvec.reshape(1, d))


def _ffn_up_kernel(h_ref, wu_ref, wv_ref, cwu_ref, cwv_ref, cbu_ref, cbv_ref, wd_ref, o_ref,
                   wdb_ref, wub_ref, wvb_ref, carry_ref, *, tiles_per_batch, nsub):
    i = pl.program_id(1)
    tm = h_ref.shape[0]
    wdb_ref[...] = wd_ref[...].astype(BF16)

    @pl.when(i == 0)
    def _():
        wub_ref[...] = wu_ref[...].astype(BF16)
        wvb_ref[...] = wv_ref[...].astype(BF16)

    first = (i % tiles_per_batch) == 0
    prevs = []
    for stream in range(2):
        pv = carry_ref[stream]
        prevs.append(jnp.where(first, jnp.zeros_like(pv), pv))
    ts = tm // nsub
    for sb in range(nsub):
        rows = slice(sb * ts, (sb + 1) * ts)
        h = h_ref[rows, :]
        conv = []
        for stream, (wb_ref, cw_ref, cb_ref) in enumerate(((wub_ref, cwu_ref, cbu_ref),
                                                            (wvb_ref, cwv_ref, cbv_ref))):
            a = jnp.dot(h, wb_ref[...], preferred_element_type=F32)
            ext = jnp.concatenate([prevs[stream], a], axis=0)
            prevs[stream] = a[ts - SUBLANES:]
            cw = cw_ref[...]
            out = cw[CONV_WIDTH - 1:CONV_WIDTH] * a
            for lag in range(1, CONV_WIDTH):
                tap = CONV_WIDTH - 1 - lag
                out = out + cw[tap:tap + 1] * pltpu.roll(ext, lag, 0)[SUBLANES:]
            conv.append(out + cb_ref[...])
        cu, cv = conv
        o_ref[rows, :] = (cu * jax.nn.sigmoid(cu) * cv).astype(o_ref.dtype)
    for stream in range(2):
        carry_ref[stream] = prevs[stream]


def _ffn_up(h, w_up, conv_w, conv_b, w_down, layer, seq):
    n, d = h.shape
    ff = w_up.shape[2] // 2
    ts = _tile(seq, FFN_ROWS, BF16_ROWS)
    nsub = FFN_ROW_BLOCKS if seq % (FFN_ROW_BLOCKS * ts) == 0 else 1
    tm = nsub * ts
    tf = _tile(ff, HIDDEN_BLOCK, LANES)
    nf = ff // tf
    ntile = n // tm
    wrows = ff // (nf * ntile)
    assert wrows * nf * ntile == ff and wrows % BF16_ROWS == 0
    cb = conv_b.astype(F32).reshape(conv_b.shape[0], 1, 2 * ff)
    return pl.pallas_call(
        functools.partial(_ffn_up_kernel, tiles_per_batch=seq // tm, nsub=nsub),
        out_shape=[jax.ShapeDtypeStruct((n, ff), BF16),
                   jax.ShapeDtypeStruct((ff, w_down.shape[2]), BF16)],
        grid=(nf, ntile),
        in_specs=[pl.BlockSpec((tm, d), lambda f, i: (i, 0)),
                  pl.BlockSpec((None, d, tf), lambda f, i: (layer, 0, f)),
                  pl.BlockSpec((None, d, tf), lambda f, i: (layer, 0, nf + f)),
                  pl.BlockSpec((None, CONV_WIDTH, tf), lambda f, i: (layer, 0, f)),
                  pl.BlockSpec((None, CONV_WIDTH, tf), lambda f, i: (layer, 0, nf + f)),
                  pl.BlockSpec((None, 1, tf), lambda f, i: (layer, 0, f)),
                  pl.BlockSpec((None, 1, tf), lambda f, i: (layer, 0, nf + f)),
                  pl.BlockSpec((None, wrows, w_down.shape[2]), lambda f, i: (layer, f * ntile + i, 0))],
        out_specs=[pl.BlockSpec((tm, tf), lambda f, i: (i, f)),
                   pl.BlockSpec((wrows, w_down.shape[2]), lambda f, i: (f * ntile + i, 0))],
        scratch_shapes=[pltpu.VMEM((d, tf), BF16), pltpu.VMEM((d, tf), BF16),
                        pltpu.VMEM((2, SUBLANES, tf), F32)],
        compiler_params=_params("arbitrary", "arbitrary"),
        name="ffn_up_conv",
    )(h, w_up, w_up, conv_w.astype(F32), conv_w.astype(F32), cb, cb, w_down)


def _ffn_down_kernel(a_ref, w_ref, x_ref, gate_ref, *rest, kinds, emit_x, with_flog):
    na = _norm_arg_count(kinds)
    norm_refs = rest[:na]
    rest = rest[na:]
    if with_flog:
        wf_ref, rest = rest[0], rest[1:]
    outs = rest
    acc_ref = outs[0]
    k = pl.program_id(1)

    def contrib():
        return jnp.dot(a_ref[...], w_ref[...], preferred_element_type=F32)

    @pl.when(k == 0)
    def _():
        acc_ref[...] = contrib()

    @pl.when(k > 0)
    def _():
        acc_ref[...] += contrib()

    @pl.when(k == pl.num_programs(1) - 1)
    def _():
        xn = x_ref[...] + gate_ref[0] * acc_ref[...]
        q = 0
        if emit_x:
            outs[0][...] = xn
            q = 1
        first = _emit_norms(xn, kinds, norm_refs, outs[q:q + len(kinds)])
        if with_flog:
            hi = first.astype(BF16)
            lo = (first - hi.astype(F32)).astype(BF16)
            wf = wf_ref[...]
            whi = wf.astype(BF16)
            wlo = (wf - whi.astype(F32)).astype(BF16)
            nt = (((1,), (1,)), ((), ()))
            nh = wf.shape[0]
            both = lax.dot_general(jnp.concatenate([whi, wlo], axis=0), hi, nt,
                                   preferred_element_type=F32)
            outs[-1][...] = (both[:nh] + both[nh:]
                             + lax.dot_general(whi, lo, nt, preferred_element_type=F32))


def _ffn_down(act, x2, w_down, gate, kinds, norms, seq, *, emit_x, wf_t=None):
    n, ff = act.shape
    d = w_down.shape[1]
    tm = _tile(seq, ROW_TILE, BF16_ROWS)
    tk = _tile(ff, FFN_DOWN_K, LANES)
    tpb = seq // tm
    row = lambda i, k: (i, 0)
    bidx = lambda i, k: (i // tpb, 0, 0)
    nspecs, nargs = _norm_specs_args(kinds, norms, d, bidx)
    oshapes, ospecs = _norm_outs(kinds, n, d, tm, row)
    if emit_x:
        oshapes = [jax.ShapeDtypeStruct((n, d), F32)] + oshapes
        ospecs = [pl.BlockSpec((tm, d), row)] + ospecs
    assert oshapes[0].dtype == F32
    extra_specs, extra_args = [], []
    if wf_t is not None:
        nh = wf_t.shape[0]
        extra_specs = [pl.BlockSpec((nh, d), lambda i, k: (0, 0))]
        extra_args = [wf_t]
        oshapes = oshapes + [jax.ShapeDtypeStruct((nh, n), F32)]
        ospecs = ospecs + [pl.BlockSpec((nh, tm), lambda i, k: (0, i))]
    return pl.pallas_call(
        functools.partial(_ffn_down_kernel, kinds=tuple(kinds), emit_x=emit_x,
                          with_flog=wf_t is not None),
        out_shape=oshapes,
        grid=(n // tm, ff // tk),
        in_specs=[pl.BlockSpec((tm, tk), lambda i, k: (i, k)),
                  pl.BlockSpec((tk, d), lambda i, k: (k, 0)),
                  pl.BlockSpec((tm, d), row),
                  pl.BlockSpec((1, 1, d), bidx)] + nspecs + extra_specs,
        out_specs=ospecs,
        compiler_params=_params("arbitrary", "arbitrary"),
        name="ffn_down_res",
    )(act, w_down, x2, gate, *nargs, *extra_args)


def _fcum_kernel(z_ref, b_ref, tri_ref, o_ref):
    z = z_ref[...] + b_ref[...]
    lf = jnp.minimum(z, 0.0) - jnp.log1p(jnp.exp(-jnp.abs(z)))
    carry = jnp.zeros((z.shape[0], 1), F32)
    for cidx in range(z.shape[1] // LANES):
        sl = slice(cidx * LANES, (cidx + 1) * LANES)
        cs = jnp.dot(lf[:, sl], tri_ref[...], precision=lax.Precision.HIGHEST,
                     preferred_element_type=F32) + carry
        o_ref[:, sl] = cs * LOG2E
        carry = cs[:, LANES - 1:LANES]


def _fcum(flog_t, forget_b, bsz, seq):
    nh, n = flog_t.shape
    tri = (jnp.arange(LANES)[:, None] <= jnp.arange(LANES)[None, :]).astype(F32)
    return pl.pallas_call(
        _fcum_kernel,
        out_shape=jax.ShapeDtypeStruct((nh, n), F32),
        grid=(bsz,),
        in_specs=[pl.BlockSpec((nh, seq), lambda b: (0, b)),
                  pl.BlockSpec((nh, 1), lambda b: (0, 0)),
                  pl.BlockSpec((LANES, LANES), lambda b: (0, 0))],
        out_specs=pl.BlockSpec((nh, seq), lambda b: (0, b)),
        compiler_params=_params("arbitrary"),
        name="forget_cumsum",
    )(flog_t, forget_b.astype(F32).reshape(nh, 1), tri)


def _attn_kernel(q_ref, k_ref, v_ref, f_ref, mask_ref, o_ref, m_ref, l_ref, acc_ref, *, tk, heads):
    qi = pl.program_id(2)
    m_ref[...] = jnp.full_like(m_ref, -jnp.inf)
    l_ref[...] = jnp.zeros_like(l_ref)
    acc_ref[...] = jnp.zeros_like(acc_ref)
    ncol = tk // LANES

    def head_step(hd, cidx, masked):
        hs = slice(hd * HEAD_DIM, (hd + 1) * HEAD_DIM)
        k0 = pl.multiple_of(cidx * tk, tk)
        kc = k_ref[pl.ds(k0, tk), hs]
        vc = v_ref[pl.ds(k0, tk), hs]
        s = lax.dot_general(q_ref[:, hs], kc, (((1,), (1,)), ((), ())), preferred_element_type=F32)
        s = s - f_ref[hd, :, pl.ds(k0, tk)]
        if masked:
            s = s + mask_ref[...]
        cols_s = [s[:, j * LANES:(j + 1) * LANES] for j in range(ncol)]
        mc = cols_s[0]
        for sj in cols_s[1:]:
            mc = jnp.maximum(mc, sj)
        m_old = m_ref[hd]
        m_new = jnp.maximum(m_old, jnp.max(mc, axis=-1, keepdims=True))
        alpha = jnp.exp2(m_old - m_new)
        ps = [jnp.exp2(sj - m_new) for sj in cols_s]
        lsum = ps[0]
        for pj in ps[1:]:
            lsum = lsum + pj
        l_ref[hd] = alpha * l_ref[hd] + lsum
        p = jnp.concatenate(ps, axis=-1).astype(BF16)
        acc_ref[hd] = alpha * acc_ref[hd] + jnp.dot(p, vc, preferred_element_type=F32)
        m_ref[hd] = m_new

    def step(cidx, masked):
        for hd in range(heads):
            head_step(hd, cidx, masked)

    n_full = qi

    def pair_body(pidx, carry):
        step(2 * pidx, False)
        step(2 * pidx + 1, False)
        return carry

    lax.fori_loop(0, n_full // 2, pair_body, 0)

    @pl.when(n_full % 2 == 1)
    def _():
        step(n_full - 1, False)

    step(n_full, True)
    for hd in range(heads):
        l_row = jnp.sum(l_ref[hd], axis=-1, keepdims=True)
        o_ref[:, hd * HEAD_DIM:(hd + 1) * HEAD_DIM] = (acc_ref[hd] / l_row).astype(o_ref.dtype)


def _attention(q, kv, f_cum, bsz, seq):
    n, d = q.shape
    nh = d // HEAD_DIM
    heads = next(c for c in (ATTN_HEADS, 2, 1) if nh % c == 0)
    tq = _tile(seq, ROW_TILE, LANES)
    tk = tq
    nq = seq // tq
    width = heads * HEAD_DIM
    ngrp = nh // heads
    causal = jnp.where(jnp.arange(tk)[None, :] <= jnp.arange(tq)[:, None], 0.0, -jnp.inf).astype(F32)
    return pl.pallas_call(
        functools.partial(_attn_kernel, tk=tk, heads=heads),
        out_shape=jax.ShapeDtypeStruct((n, d), BF16),
        grid=(bsz, ngrp, nq),
        in_specs=[pl.BlockSpec((tq, width), lambda b, h, i: (b * nq + i, h)),
                  pl.BlockSpec((seq, width), lambda b, h, i: (b, h)),
                  pl.BlockSpec((seq, width), lambda b, h, i: (b, ngrp + h)),
                  pl.BlockSpec((heads, 1, seq), lambda b, h, i: (h, 0, b)),
                  pl.BlockSpec((tq, tk), lambda b, h, i: (0, 0))],
        out_specs=pl.BlockSpec((tq, width), lambda b, h, i: (b * nq + i, h)),
        scratch_shapes=[pltpu.VMEM((heads, tq, LANES), F32), pltpu.VMEM((heads, tq, LANES), F32),
                        pltpu.VMEM((heads, tq, HEAD_DIM), F32)],
        compiler_params=_params("arbitrary", "arbitrary", "arbitrary"),
        name="fox_attention",
    )(q, kv, kv, f_cum.reshape(nh, 1, n), causal)


def kernel(x, c, mod_w, mod_b, norm_mix_g, norm_ffn_g, ssm_w_in, ssm_log_step, ssm_a_re, ssm_a_im,
           ssm_b_re, ssm_b_im, ssm_c_re, ssm_c_im, ssm_d, ssm_glu_w, ssm_glu_b, ssm_w_out,
           kv_mod_w, kv_mod_b, kv_norm_g, kv_w, forget_b, attn_w_q, attn_w_out,
           ffn_w_up, ffn_conv_w, ffn_conv_b, ffn_w_down, final_norm_g):
    bsz, seq, d = x.shape
    n = bsz * seq
    depth = mod_w.shape[0]
    n_a = ssm_w_in.shape[0]
    assert n_a >= 1 and d % LANES == 0
    x2 = x.astype(F32).reshape(n, d)

    c_rep = jnp.broadcast_to(c.astype(F32)[:, :, None], (bsz, d, LANES))

    def vecs(m, count):
        return [m[:, k * d:(k + 1) * d].reshape(bsz, 1, d) for k in range(count)]

    mods = [vecs(_gemv(c_rep, mod_w, i, mod_b[i]), N_MOD) for i in range(depth)]
    sh_kv, sc_kv = vecs(_gemv(c_rep, kv_mod_w.reshape(1, *kv_mod_w.shape), 0, kv_mod_b), 2)

    def mix_norm(i):
        return (norm_mix_g[i], mods[i][0], mods[i][1])

    def after_ffn(i):
        if i + 1 == depth:
            return ["plain"], [(final_norm_g,)]
        kinds, norms = [], []
        if i + 1 == n_a:
            kinds.append("mod")
            norms.append((kv_norm_g, sh_kv, sc_kv))
        kinds.append("mod")
        norms.append(mix_norm(i + 1))
        return kinds, norms

    h = _norm_mod(x2, *mix_norm(0), seq)
    kv = f_cum = None
    hk = flog_t = None
    out = None
    for i in range(depth):
        sh_m, sc_m, g_m, sh_f, sc_f, g_f = mods[i]
        ffn_norm = (norm_ffn_g[i], sh_f, sc_f)
        if i == n_a:
            kv = _mm(hk, kv_w, 0, 2 * d, name="kv_proj")
            f_cum = _fcum(flog_t, forget_b, bsz, seq)
        if i < n_a:
            u = _mm(h, ssm_w_in, i, d, name="ssm_in_proj")
            bpack, cpack, coef = _s5_pack(ssm_log_step[i], ssm_a_re[i], ssm_a_im[i], ssm_b_re[i],
                                          ssm_b_im[i], ssm_c_re[i], ssm_c_im[i], _ssm_chunk(seq)[1])
            g = _ssm(u, bpack, cpack, coef, ssm_d[i].astype(F32), bsz, seq)
            z = _mm(g, ssm_glu_w, i, d, glu_bias=ssm_glu_b[i].astype(F32), name="ssm_glu")
            x2, h = _mm_res(z, ssm_w_out, i, x2, g_m, ["mod"], [ffn_norm], seq, name="ssm_out_proj")
        else:
            j = i - n_a
            q = _mm(h, attn_w_q, j, d, out_scale=HEAD_DIM ** -0.5 * LOG2E, name="q_proj")
            o = _attention(q, kv, f_cum, bsz, seq)
            x2, h = _mm_res(o, attn_w_out, j, x2, g_m, ["mod"], [ffn_norm], seq, name="attn_out_proj")
        kinds, norms = after_ffn(i)
        last = i + 1 == depth
        want_kv = (i + 1 == n_a) and not last
        wf_t = kv_w[:, 2 * d:].astype(F32).T if want_kv else None
        act, w_down_bf = _ffn_up(h, ffn_w_up, ffn_conv_w, ffn_conv_b, ffn_w_down, i, seq)
        res = _ffn_down(act, x2, w_down_bf, g_f, kinds, norms, seq, emit_x=not last, wf_t=wf_t)
        if last:
            out = res[0]
        else:
            x2 = res[0]
            if want_kv:
                hk, h, flog_t = res[1], res[2], res[3]
            else:
                h = res[1]
    return out.reshape(bsz, seq, d).astype(x.dtype)
```

```python
import functools
import math

import jax
import jax.numpy as jnp
import numpy as np
from jax import lax
from jax.experimental import pallas as pl
from jax.experimental.pallas import tpu as pltpu

F32 = jnp.float32
BF16 = jnp.bfloat16

LANES = 128
SUBLANES = 8
BF16_ROWS = 16
VMEM_LIMIT_BYTES = 56 * 1024 * 1024

SSM_GROUP = 16
SSM_STATE = 64
HEAD_DIM = 128
CONV_WIDTH = 3
NORM_EPS = 1e-6
LOG2E = math.log2(math.e)
N_MOD = 6

GROUPS_PER_BLOCK = LANES // SSM_GROUP
STATE_LANES = GROUPS_PER_BLOCK * SSM_STATE
SCAN_STEPS = (1, 2, 4)
SSM_CHUNK = 512
SSM_SLABS = 8
ATTN_HEADS = 4
FFN_ROWS = 1024
FFN_ROW_BLOCKS = 1
ROW_TILE = 512
MM_TILE = 1024
HIDDEN_BLOCK = 512
FFN_DOWN_K = 1536


def _params(*sem):
    return pltpu.CompilerParams(dimension_semantics=sem, vmem_limit_bytes=VMEM_LIMIT_BYTES)


def _tile(n, pref, align):
    if n <= pref:
        return n
    t = (pref // align) * align
    while t >= align:
        if n % t == 0:
            return t
        t -= align
    raise ValueError(f"no tile for {n}")


def _rms_hat(x):
    ms = jnp.mean(x * x, axis=-1, keepdims=True)
    return x * lax.rsqrt(ms + NORM_EPS)


def _modulate(y, shift, scale):
    return y * (1 + scale) + shift


def _gemv_kernel(c_ref, w_ref, b_ref, o_ref, cs_ref):
    @pl.when(pl.program_id(0) == 0)
    def _():
        cv = c_ref[...]
        cs_ref[...] = cv * jax.nn.sigmoid(cv)

    tn = w_ref.shape[1]
    for b in range(c_ref.shape[0]):
        cb = cs_ref[b]
        for lt in range(tn // LANES):
            sl = slice(lt * LANES, (lt + 1) * LANES)
            o_ref[b:b + 1, sl] = jnp.sum(w_ref[:, sl] * cb, axis=0, keepdims=True) + b_ref[:, sl]


def _gemv(c_rep, w_stack, layer, b):
    bsz, k, _ = c_rep.shape
    n = w_stack.shape[2]
    tn = _tile(n, MM_TILE, LANES)
    return pl.pallas_call(
        _gemv_kernel,
        out_shape=jax.ShapeDtypeStruct((bsz, n), F32),
        grid=(n // tn,),
        in_specs=[pl.BlockSpec((bsz, k, LANES), lambda j: (0, 0, 0)),
                  pl.BlockSpec((None, k, tn), lambda j: (layer, 0, j)),
                  pl.BlockSpec((1, tn), lambda j: (0, j))],
        out_specs=pl.BlockSpec((bsz, tn), lambda j: (0, j)),
        scratch_shapes=[pltpu.VMEM((bsz, k, LANES), F32)],
        compiler_params=_params("arbitrary"),
        name="adaln_gemv",
    )(c_rep, w_stack, b.reshape(1, n))


def _norm_mod_kernel(x_ref, g_ref, sh_ref, sc_ref, o_ref):
    y = _rms_hat(x_ref[...]) * g_ref[...]
    o_ref[...] = _modulate(y, sh_ref[0], sc_ref[0]).astype(o_ref.dtype)


def _norm_mod(x2, g, shift, scale, seq):
    n, d = x2.shape
    tm = _tile(seq, MM_TILE, BF16_ROWS)
    tpb = seq // tm
    vec = pl.BlockSpec((1, 1, d), lambda i: (i // tpb, 0, 0))
    return pl.pallas_call(
        _norm_mod_kernel,
        out_shape=jax.ShapeDtypeStruct((n, d), BF16),
        grid=(n // tm,),
        in_specs=[pl.BlockSpec((tm, d), lambda i: (i, 0)),
                  pl.BlockSpec((1, d), lambda i: (0, 0)), vec, vec],
        out_specs=pl.BlockSpec((tm, d), lambda i: (i, 0)),
        compiler_params=_params("arbitrary"),
        name="norm_mod",
    )(x2, g.reshape(1, d), shift, scale)


def _mm_kernel(a_ref, w_ref, *rest, glu, out_scale):
    *rest, wb_ref = rest

    @pl.when(pl.program_id(1) == 0)
    def _():
        wb_ref[...] = w_ref[...].astype(BF16)

    acc = jnp.dot(a_ref[...], wb_ref[...], preferred_element_type=F32)
    if glu:
        g_ref, b_ref, o_ref = rest
        o_ref[...] = (g_ref[...].astype(F32) * jax.nn.sigmoid(acc + b_ref[...])).astype(o_ref.dtype)
    else:
        (o_ref,) = rest
        if out_scale != 1.0:
            acc = acc * out_scale
        o_ref[...] = acc.astype(o_ref.dtype)


def _mm(a, w_stack, layer, nout, *, glu_bias=None, out_scale=1.0, name):
    n, k = a.shape
    tm = _tile(n, MM_TILE, BF16_ROWS)
    tn = _tile(nout, MM_TILE, LANES)
    if w_stack.ndim == 2:
        w_spec = pl.BlockSpec((k, tn), lambda j, i: (0, j))
    else:
        w_spec = pl.BlockSpec((None, k, tn), lambda j, i: (layer, 0, j))
    in_specs = [pl.BlockSpec((tm, k), lambda j, i: (i, 0)), w_spec]
    args = [a, w_stack]
    if glu_bias is not None:
        in_specs += [pl.BlockSpec((tm, tn), lambda j, i: (i, j)),
                     pl.BlockSpec((1, tn), lambda j, i: (0, j))]
        args += [a, glu_bias.reshape(1, nout)]
    return pl.pallas_call(
        functools.partial(_mm_kernel, glu=glu_bias is not None, out_scale=out_scale),
        out_shape=jax.ShapeDtypeStruct((n, nout), BF16),
        grid=(nout // tn, n // tm),
        in_specs=in_specs,
        out_specs=pl.BlockSpec((tm, tn), lambda j, i: (i, j)),
        scratch_shapes=[pltpu.VMEM((k, tn), BF16)],
        compiler_params=_params("arbitrary", "arbitrary"),
        name=name,
    )(*args)


def _norm_arg_count(kinds):
    return sum(3 if kd == "mod" else 1 for kd in kinds)


def _emit_norms(xn, kinds, norm_refs, out_refs):
    xhat = _rms_hat(xn)
    first = None
    p = 0
    for kd, o_ref in zip(kinds, out_refs):
        if kd == "mod":
            g_ref, sh_ref, sc_ref = norm_refs[p:p + 3]
            p += 3
            val = _modulate(xhat * g_ref[...], sh_ref[0], sc_ref[0])
        else:
            val = xhat * norm_refs[p][...]
            p += 1
        if first is None:
            first = val
        o_ref[...] = val.astype(o_ref.dtype)
    return first


def _norm_specs_args(kinds, norms, d, bidx):
    specs, args = [], []
    for kd, nm in zip(kinds, norms):
        specs.append(pl.BlockSpec((1, d), lambda *ix: (0, 0)))
        args.append(nm[0].reshape(1, d))
        if kd == "mod":
            for v in nm[1:]:
                specs.append(pl.BlockSpec((1, 1, d), bidx))
                args.append(v)
    return specs, args


def _norm_outs(kinds, n, d, tm, row_map):
    shapes = [jax.ShapeDtypeStruct((n, d), BF16 if kd == "mod" else F32) for kd in kinds]
    specs = [pl.BlockSpec((tm, d), row_map) for _ in kinds]
    return shapes, specs


def _mm_res_kernel(a_ref, w_ref, x_ref, gate_ref, *rest, kinds):
    *rest, wb_ref = rest
    na = _norm_arg_count(kinds)
    norm_refs, outs = rest[:na], rest[na:]

    @pl.when(pl.program_id(0) == 0)
    def _():
        wb_ref[...] = w_ref[...].astype(BF16)

    acc = jnp.dot(a_ref[...], wb_ref[...], preferred_element_type=F32)
    xn = x_ref[...] + gate_ref[0] * acc
    outs[0][...] = xn
    _emit_norms(xn, kinds, norm_refs, outs[1:])


def _mm_res(a, w_stack, layer, x2, gate, kinds, norms, seq, *, name):
    n, k = a.shape
    d = w_stack.shape[2]
    tm = _tile(seq, ROW_TILE, BF16_ROWS)
    tpb = seq // tm
    bidx = lambda i: (i // tpb, 0, 0)
    row = lambda i: (i, 0)
    nspecs, nargs = _norm_specs_args(kinds, norms, d, bidx)
    oshapes, ospecs = _norm_outs(kinds, n, d, tm, row)
    return pl.pallas_call(
        functools.partial(_mm_res_kernel, kinds=tuple(kinds)),
        out_shape=[jax.ShapeDtypeStruct((n, d), F32)] + oshapes,
        grid=(n // tm,),
        in_specs=[pl.BlockSpec((tm, k), row),
                  pl.BlockSpec((None, k, d), lambda i: (layer, 0, 0), pipeline_mode=pl.Buffered(1)),
                  pl.BlockSpec((tm, d), row),
                  pl.BlockSpec((1, 1, d), bidx)] + nspecs,
        out_specs=[pl.BlockSpec((tm, d), row)] + ospecs,
        scratch_shapes=[pltpu.VMEM((k, d), BF16)],
        compiler_params=_params("arbitrary"),
        name=name,
    )(a, w_stack, x2, gate, *nargs)


def _ssm_kernel(u_ref, bp_ref, cp_ref, coef_ref, d_ref, o_ref, upad_ref, st_ref, ypad_ref,
                carry_ref, *, slabs, seg_len):
    s = STATE_LANES
    nseg = SUBLANES
    pitch = seg_len + 1
    nlt = s // LANES

    @pl.when(pl.program_id(2) == 0)
    def _():
        carry_ref[...] = jnp.zeros_like(carry_ref)

    def cmul(ar, ai, xr, xi):
        return ar * xr - ai * xi, ar * xi + ai * xr

    def load_rows(sb, half, rows):
        return jnp.concatenate([st_ref[sb, half * nlt + j, rows, :] for j in range(nlt)], axis=-1)

    def store_rows(sb, half, rows, val):
        for j in range(nlt):
            st_ref[sb, half * nlt + j, rows, :] = val[:, j * LANES:(j + 1) * LANES]

    for sb in range(slabs):
        ch = slice(sb * LANES, (sb + 1) * LANES)
        u32 = u_ref[:, ch].astype(F32)
        upad_ref[sb] = jnp.zeros(upad_ref.shape[1:], F32)
        for k in range(nseg):
            upad_ref[sb, pl.ds(k * pitch, seg_len), :] = u32[k * seg_len:(k + 1) * seg_len]
        bu = jnp.dot(upad_ref[sb].astype(BF16), bp_ref[sb], preferred_element_type=F32)
        for j in range(2 * nlt):
            st_ref[sb, j] = bu[:, j * LANES:(j + 1) * LANES]
        ar, ai = coef_ref[sb, 0], coef_ref[sb, 1]

        sr = jnp.zeros((nseg, s), F32)
        si = jnp.zeros((nseg, s), F32)
        for t in range(seg_len):
            rows = pl.ds(t, nseg, stride=pitch)
            pr_, pi_ = cmul(ar, ai, sr, si)
            sr = pr_ + load_rows(sb, 0, rows)
            si = pi_ + load_rows(sb, 1, rows)
            store_rows(sb, 0, rows, sr)
            store_rows(sb, 1, rows, si)

        tr, ti = sr, si
        for q, dlt in enumerate(SCAN_STEPS):
            hr, hi = coef_ref[sb, 2 + 2 * q], coef_ref[sb, 3 + 2 * q]
            dr, di = cmul(hr, hi, pltpu.roll(tr, dlt, 0), pltpu.roll(ti, dlt, 0))
            tr, ti = tr + dr, ti + di
        cr, ci = carry_ref[sb, 0], carry_ref[sb, 1]
        dr, di = cmul(coef_ref[sb, 8], coef_ref[sb, 9], cr, ci)
        tr, ti = tr + dr, ti + di
        row0 = lax.broadcasted_iota(jnp.int32, (nseg, s), 0) == 0
        zr = jnp.where(row0, cr, pltpu.roll(tr, 1, 0))
        zi = jnp.where(row0, ci, pltpu.roll(ti, 1, 0))
        carry_ref[sb, 0] = jnp.broadcast_to(tr[nseg - 1:nseg], tr.shape)
        carry_ref[sb, 1] = jnp.broadcast_to(ti[nseg - 1:nseg], ti.shape)

        for t in range(seg_len):
            rows = pl.ds(t, nseg, stride=pitch)
            zr, zi = cmul(ar, ai, zr, zi)
            store_rows(sb, 0, rows, load_rows(sb, 0, rows) + zr)
            store_rows(sb, 1, rows, load_rows(sb, 1, rows) + zi)

        states = jnp.concatenate([st_ref[sb, j] for j in range(2 * nlt)], axis=-1).astype(BF16)
        ypad_ref[sb] = jnp.dot(states, cp_ref[sb], preferred_element_type=F32)
        y = jnp.concatenate([ypad_ref[sb, pl.ds(k * pitch, seg_len), :] for k in range(nseg)], axis=0)
        y = y + d_ref[:, ch] * u32
        o_ref[:, ch] = jax.nn.gelu(y).astype(o_ref.dtype)


def _ssm_chunk(seq):
    tt = _tile(seq, SSM_CHUNK, SUBLANES * SUBLANES)
    return tt, tt // SUBLANES


def _s5_pack(log_step, a_re, a_im, b_re, b_im, c_re, c_im, seg_len):
    g, p = a_re.shape
    h = b_re.shape[-1]
    nblk = g // GROUPS_PER_BLOCK
    step = jnp.exp(log_step.astype(F32))[:, None]
    lam_re, lam_im = a_re.astype(F32), a_im.astype(F32)

    rows = np.arange(SUBLANES)
    expo = np.stack([np.ones(SUBLANES)] + [np.full(SUBLANES, seg_len * dlt) for dlt in SCAN_STEPS]
                    + [seg_len * (rows + 1.0)]).astype(np.float32)
    keep = np.stack([np.ones(SUBLANES)] + [rows >= dlt for dlt in SCAN_STEPS]
                    + [np.ones(SUBLANES)]).astype(np.float32)
    kk = expo[:, :, None, None]
    mag = jnp.exp(kk * (lam_re * step)) * keep[:, :, None, None]
    ang = kk * (lam_im * step)
    pow_re, pow_im = mag * jnp.cos(ang), mag * jnp.sin(ang)
    coef = jnp.stack([pow_re, pow_im], axis=1).reshape(len(expo), 2, SUBLANES, nblk, STATE_LANES)
    coef = coef.transpose(3, 0, 1, 2, 4).reshape(nblk, 2 * len(expo), SUBLANES, STATE_LANES)

    abar_re, abar_im = pow_re[0, 0], pow_im[0, 0]
    den = lam_re * lam_re + lam_im * lam_im
    nr = abar_re - 1.0
    fr = (nr * lam_re + abar_im * lam_im) / den
    fi = (abar_im * lam_re - nr * lam_im) / den
    br, bi = b_re.astype(F32), b_im.astype(F32)
    bbar_re = fr[..., None] * br - fi[..., None] * bi
    bbar_im = fr[..., None] * bi + fi[..., None] * br
    eye = jnp.eye(GROUPS_PER_BLOCK, dtype=F32)
    bb = jnp.stack([bbar_re, bbar_im]).reshape(2, nblk, GROUPS_PER_BLOCK, p, h)
    bpack = jnp.einsum("zjgph,gk->jghzkp", bb, eye).reshape(nblk, LANES, 2 * STATE_LANES)
    cc = jnp.stack([c_re.astype(F32), -c_im.astype(F32)]).reshape(2, nblk, GROUPS_PER_BLOCK, h, p)
    cpack = jnp.einsum("zjghp,gk->jzgpkh", cc, eye).reshape(nblk, 2 * STATE_LANES, LANES)
    return bpack.astype(BF16), cpack.astype(BF16), coef


def _ssm(u, bpack, cpack, coef, dvec, bsz, seq):
    n, d = u.shape
    nblk = d // LANES
    tt, seg_len = _ssm_chunk(seq)
    nt = seq // tt
    slabs = next(c for c in (SSM_SLABS, 2, 1) if nblk % c == 0)
    width = slabs * LANES
    prow = -(-(SUBLANES * (seg_len + 1)) // BF16_ROWS) * BF16_ROWS
    row = lambda b, j, t: (b * nt + t, j)
    return pl.pallas_call(
        functools.partial(_ssm_kernel, slabs=slabs, seg_len=seg_len),
        out_shape=jax.ShapeDtypeStruct((n, d), BF16),
        grid=(bsz, nblk // slabs, nt),
        in_specs=[pl.BlockSpec((tt, width), row),
                  pl.BlockSpec((slabs, LANES, 2 * STATE_LANES), lambda b, j, t: (j, 0, 0)),
                  pl.BlockSpec((slabs, 2 * STATE_LANES, LANES), lambda b, j, t: (j, 0, 0)),
                  pl.BlockSpec((slabs, 2 * len(SCAN_STEPS) + 4, SUBLANES, STATE_LANES),
                               lambda b, j, t: (j, 0, 0, 0)),
                  pl.BlockSpec((1, width), lambda b, j, t: (0, j))],
        out_specs=pl.BlockSpec((tt, width), row),
        scratch_shapes=[pltpu.VMEM((slabs, prow, LANES), F32),
                        pltpu.VMEM((slabs, 2 * STATE_LANES // LANES, prow, LANES), F32),
                        pltpu.VMEM((slabs, prow, LANES), F32),
                        pltpu.VMEM((slabs, 2, SUBLANES, STATE_LANES), F32)],
        compiler_params=_params("arbitrary", "arbitrary", "arbitrary"),
        name="s5_ssm",
    )(u, bpack, cpack, coef, dvec.reshape(1, d))


def _ffn_up_kernel(h_ref, wu_ref, wv_ref, cwu_ref, cwv_ref, cbu_ref, cbv_ref, wd_ref, o_ref,
                   wdb_ref, wub_ref, wvb_ref, carry_ref, *, tiles_per_batch, nsub):
    i = pl.program_id(1)
    tm = h_ref.shape[0]
    wdb_ref[...] = wd_ref[...].astype(BF16)

    @pl.when(i == 0)
    def _():
        wub_ref[...] = wu_ref[...].astype(BF16)
        wvb_ref[...] = wv_ref[...].astype(BF16)

    first = (i % tiles_per_batch) == 0
    prevs = []
    for stream in range(2):
        pv = carry_ref[stream]
        prevs.append(jnp.where(first, jnp.zeros_like(pv), pv))
    ts = tm // nsub
    for sb in range(nsub):
        rows = slice(sb * ts, (sb + 1) * ts)
        h = h_ref[rows, :]
        conv = []
        for stream, (wb_ref, cw_ref, cb_ref) in enumerate(((wub_ref, cwu_ref, cbu_ref),
                                                            (wvb_ref, cwv_ref, cbv_ref))):
            a = jnp.dot(h, wb_ref[...], preferred_element_type=F32)
            ext = jnp.concatenate([prevs[stream], a], axis=0)
            prevs[stream] = a[ts - SUBLANES:]
            cw = cw_ref[...]
            out = cw[CONV_WIDTH - 1:CONV_WIDTH] * a
            for lag in range(1, CONV_WIDTH):
                tap = CONV_WIDTH - 1 - lag
                out = out + cw[tap:tap + 1] * pltpu.roll(ext, lag, 0)[SUBLANES:]
            conv.append(out + cb_ref[...])
        cu, cv = conv
        o_ref[rows, :] = (cu * jax.nn.sigmoid(cu) * cv).astype(o_ref.dtype)
    for stream in range(2):
        carry_ref[stream] = prevs[stream]


def _ffn_up(h, w_up, conv_w, conv_b, w_down, layer, seq):
    n, d = h.shape
    ff = w_up.shape[2] // 2
    ts = _tile(seq, FFN_ROWS, BF16_ROWS)
    nsub = FFN_ROW_BLOCKS if seq % (FFN_ROW_BLOCKS * ts) == 0 else 1
    tm = nsub * ts
    tf = _tile(ff, HIDDEN_BLOCK, LANES)
    nf = ff // tf
    ntile = n // tm
    wrows = ff // (nf * ntile)
    assert wrows * nf * ntile == ff and wrows % BF16_ROWS == 0
    cb = conv_b.astype(F32).reshape(conv_b.shape[0], 1, 2 * ff)
    return pl.pallas_call(
        functools.partial(_ffn_up_kernel, tiles_per_batch=seq // tm, nsub=nsub),
        out_shape=[jax.ShapeDtypeStruct((n, ff), BF16),
                   jax.ShapeDtypeStruct((ff, w_down.shape[2]), BF16)],
        grid=(nf, ntile),
        in_specs=[pl.BlockSpec((tm, d), lambda f, i: (i, 0)),
                  pl.BlockSpec((None, d, tf), lambda f, i: (layer, 0, f)),
                  pl.BlockSpec((None, d, tf), lambda f, i: (layer, 0, nf + f)),
                  pl.BlockSpec((None, CONV_WIDTH, tf), lambda f, i: (layer, 0, f)),
                  pl.BlockSpec((None, CONV_WIDTH, tf), lambda f, i: (layer, 0, nf + f)),
                  pl.BlockSpec((None, 1, tf), lambda f, i: (layer, 0, f)),
                  pl.BlockSpec((None, 1, tf), lambda f, i: (layer, 0, nf + f)),
                  pl.BlockSpec((None, wrows, w_down.shape[2]), lambda f, i: (layer, f * ntile + i, 0))],
        out_specs=[pl.BlockSpec((tm, tf), lambda f, i: (i, f)),
                   pl.BlockSpec((wrows, w_down.shape[2]), lambda f, i: (f * ntile + i, 0))],
        scratch_shapes=[pltpu.VMEM((d, tf), BF16), pltpu.VMEM((d, tf), BF16),
                        pltpu.VMEM((2, SUBLANES, tf), F32)],
        compiler_params=_params("arbitrary", "arbitrary"),
        name="ffn_up_conv",
    )(h, w_up, w_up, conv_w.astype(F32), conv_w.astype(F32), cb, cb, w_down)


def _ffn_down_kernel(a_ref, w_ref, x_ref, gate_ref, *rest, kinds, emit_x, with_flog):
    na = _norm_arg_count(kinds)
    norm_refs = rest[:na]
    rest = rest[na:]
    if with_flog:
        wf_ref, rest = rest[0], rest[1:]
    outs = rest
    acc_ref = outs[0]
    k = pl.program_id(1)

    def contrib():
        return jnp.dot(a_ref[...], w_ref[...], preferred_element_type=F32)

    @pl.when(k == 0)
    def _():
        acc_ref[...] = contrib()

    @pl.when(k > 0)
    def _():
        acc_ref[...] += contrib()

    @pl.when(k == pl.num_programs(1) - 1)
    def _():
        xn = x_ref[...] + gate_ref[0] * acc_ref[...]
        q = 0
        if emit_x:
            outs[0][...] = xn
            q = 1
        first = _emit_norms(xn, kinds, norm_refs, outs[q:q + len(kinds)])
        if with_flog:
            hi = first.astype(BF16)
            lo = (first - hi.astype(F32)).astype(BF16)
            wf = wf_ref[...]
            whi = wf.astype(BF16)
            wlo = (wf - whi.astype(F32)).astype(BF16)
            nt = (((1,), (1,)), ((), ()))
            nh = wf.shape[0]
            both = lax.dot_general(jnp.concatenate([whi, wlo], axis=0), hi, nt,
                                   preferred_element_type=F32)
            outs[-1][...] = (both[:nh] + both[nh:]
                             + lax.dot_general(whi, lo, nt, preferred_element_type=F32))


def _ffn_down(act, x2, w_down, gate, kinds, norms, seq, *, emit_x, wf_t=None):
    n, ff = act.shape
    d = w_down.shape[1]
    tm = _tile(seq, ROW_TILE, BF16_ROWS)
    tk = _tile(ff, FFN_DOWN_K, LANES)
    tpb = seq // tm
    row = lambda i, k: (i, 0)
    bidx = lambda i, k: (i // tpb, 0, 0)
    nspecs, nargs = _norm_specs_args(kinds, norms, d, bidx)
    oshapes, ospecs = _norm_outs(kinds, n, d, tm, row)
    if emit_x:
        oshapes = [jax.ShapeDtypeStruct((n, d), F32)] + oshapes
        ospecs = [pl.BlockSpec((tm, d), row)] + ospecs
    assert oshapes[0].dtype == F32
    extra_specs, extra_args = [], []
    if wf_t is not None:
        nh = wf_t.shape[0]
        extra_specs = [pl.BlockSpec((nh, d), lambda i, k: (0, 0))]
        extra_args = [wf_t]
        oshapes = oshapes + [jax.ShapeDtypeStruct((nh, n), F32)]
        ospecs = ospecs + [pl.BlockSpec((nh, tm), lambda i, k: (0, i))]
    return pl.pallas_call(
        functools.partial(_ffn_down_kernel, kinds=tuple(kinds), emit_x=emit_x,
                          with_flog=wf_t is not None),
        out_shape=oshapes,
        grid=(n // tm, ff // tk),
        in_specs=[pl.BlockSpec((tm, tk), lambda i, k: (i, k)),
                  pl.BlockSpec((tk, d), lambda i, k: (k, 0)),
                  pl.BlockSpec((tm, d), row),
                  pl.BlockSpec((1, 1, d), bidx)] + nspecs + extra_specs,
        out_specs=ospecs,
        compiler_params=_params("arbitrary", "arbitrary"),
        name="ffn_down_res",
    )(act, w_down, x2, gate, *nargs, *extra_args)


def _fcum_kernel(z_ref, b_ref, tri_ref, o_ref):
    z = z_ref[...] + b_ref[...]
    lf = jnp.minimum(z, 0.0) - jnp.log1p(jnp.exp(-jnp.abs(z)))
    carry = jnp.zeros((z.shape[0], 1), F32)
    for cidx in range(z.shape[1] // LANES):
        sl = slice(cidx * LANES, (cidx + 1) * LANES)
        cs = jnp.dot(lf[:, sl], tri_ref[...], precision=lax.Precision.HIGHEST,
                     preferred_element_type=F32) + carry
        o_ref[:, sl] = cs * LOG2E
        carry = cs[:, LANES - 1:LANES]


def _fcum(flog_t, forget_b, bsz, seq):
    nh, n = flog_t.shape
    tri = (jnp.arange(LANES)[:, None] <= jnp.arange(LANES)[None, :]).astype(F32)
    return pl.pallas_call(
        _fcum_kernel,
        out_shape=jax.ShapeDtypeStruct((nh, n), F32),
        grid=(bsz,),
        in_specs=[pl.BlockSpec((nh, seq), lambda b: (0, b)),
                  pl.BlockSpec((nh, 1), lambda b: (0, 0)),
                  pl.BlockSpec((LANES, LANES), lambda b: (0, 0))],
        out_specs=pl.BlockSpec((nh, seq), lambda b: (0, b)),
        compiler_params=_params("arbitrary"),
        name="forget_cumsum",
    )(flog_t, forget_b.astype(F32).reshape(nh, 1), tri)


def _attn_kernel(q_ref, k_ref, v_ref, f_ref, mask_ref, o_ref, m_ref, l_ref, acc_ref, *, tk, heads):
    qi = pl.program_id(2)
    m_ref[...] = jnp.full_like(m_ref, -jnp.inf)
    l_ref[...] = jnp.zeros_like(l_ref)
    acc_ref[...] = jnp.zeros_like(acc_ref)
    ncol = tk // LANES

    def head_step(hd, cidx, masked):
        hs = slice(hd * HEAD_DIM, (hd + 1) * HEAD_DIM)
        k0 = pl.multiple_of(cidx * tk, tk)
        kc = k_ref[pl.ds(k0, tk), hs]
        vc = v_ref[pl.ds(k0, tk), hs]
        s = lax.dot_general(q_ref[:, hs], kc, (((1,), (1,)), ((), ())), preferred_element_type=F32)
        s = s - f_ref[hd, :, pl.ds(k0, tk)]
        if masked:
            s = s + mask_ref[...]
        cols_s = [s[:, j * LANES:(j + 1) * LANES] for j in range(ncol)]
        mc = cols_s[0]
        for sj in cols_s[1:]:
            mc = jnp.maximum(mc, sj)
        m_old = m_ref[hd]
        m_new = jnp.maximum(m_old, jnp.max(mc, axis=-1, keepdims=True))
        alpha = jnp.exp2(m_old - m_new)
        ps = [jnp.exp2(sj - m_new) for sj in cols_s]
        lsum = ps[0]
        for pj in ps[1:]:
            lsum = lsum + pj
        l_ref[hd] = alpha * l_ref[hd] + lsum
        p = jnp.concatenate(ps, axis=-1).astype(BF16)
        acc_ref[hd] = alpha * acc_ref[hd] + jnp.dot(p, vc, preferred_element_type=F32)
        m_ref[hd] = m_new

    def step(cidx, masked):
        for hd in range(heads):
            head_step(hd, cidx, masked)

    n_full = qi

    def pair_body(pidx, carry):
        step(2 * pidx, False)
        step(2 * pidx + 1, False)
        return carry

    lax.fori_loop(0, n_full // 2, pair_body, 0)

    @pl.when(n_full % 2 == 1)
    def _():
        step(n_full - 1, False)

    step(n_full, True)
    for hd in range(heads):
        l_row = jnp.sum(l_ref[hd], axis=-1, keepdims=True)
        o_ref[:, hd * HEAD_DIM:(hd + 1) * HEAD_DIM] = (acc_ref[hd] / l_row).astype(o_ref.dtype)


def _attention(q, kv, f_cum, bsz, seq):
    n, d = q.shape
    nh = d // HEAD_DIM
    heads = next(c for c in (ATTN_HEADS, 2, 1) if nh % c == 0)
    tq = _tile(seq, ROW_TILE, LANES)
    tk = tq
    nq = seq // tq
    width = heads * HEAD_DIM
    ngrp = nh // heads
    causal = jnp.where(jnp.arange(tk)[None, :] <= jnp.arange(tq)[:, None], 0.0, -jnp.inf).astype(F32)
    return pl.pallas_call(
        functools.partial(_attn_kernel, tk=tk, heads=heads),
        out_shape=jax.ShapeDtypeStruct((n, d), BF16),
        grid=(bsz, ngrp, nq),
        in_specs=[pl.BlockSpec((tq, width), lambda b, h, i: (b * nq + i, h)),
                  pl.BlockSpec((seq, width), lambda b, h, i: (b, h)),
                  pl.BlockSpec((seq, width), lambda b, h, i: (b, ngrp + h)),
                  pl.BlockSpec((heads, 1, seq), lambda b, h, i: (h, 0, b)),
                  pl.BlockSpec((tq, tk), lambda b, h, i: (0, 0))],
        out_specs=pl.BlockSpec((tq, width), lambda b, h, i: (b * nq + i, h)),
        scratch_shapes=[pltpu.VMEM((heads, tq, LANES), F32), pltpu.VMEM((heads, tq, LANES), F32),
                        pltpu.VMEM((heads, tq, HEAD_DIM), F32)],
        compiler_params=_params("arbitrary", "arbitrary", "arbitrary"),
        name="fox_attention",
    )(q, kv, kv, f_cum.reshape(nh, 1, n), causal)


def kernel(x, c, mod_w, mod_b, norm_mix_g, norm_ffn_g, ssm_w_in, ssm_log_step, ssm_a_re, ssm_a_im,
           ssm_b_re, ssm_b_im, ssm_c_re, ssm_c_im, ssm_d, ssm_glu_w, ssm_glu_b, ssm_w_out,
           kv_mod_w, kv_mod_b, kv_norm_g, kv_w, forget_b, attn_w_q, attn_w_out,
           ffn_w_up, ffn_conv_w, ffn_conv_b, ffn_w_down, final_norm_g):
    bsz, seq, d = x.shape
    n = bsz * seq
    depth = mod_w.shape[0]
    n_a = ssm_w_in.shape[0]
    assert n_a >= 1 and d % LANES == 0
    x2 = x.astype(F32).reshape(n, d)

    c_rep = jnp.broadcast_to(c.astype(F32)[:, :, None], (bsz, d, LANES))

    def vecs(m, count):
        return [m[:, k * d:(k + 1) * d].reshape(bsz, 1, d) for k in range(count)]

    mods = [vecs(_gemv(c_rep, mod_w, i, mod_b[i]), N_MOD) for i in range(depth)]
    sh_kv, sc_kv = vecs(_gemv(c_rep, kv_mod_w.reshape(1, *kv_mod_w.shape), 0, kv_mod_b), 2)

    def mix_norm(i):
        return (norm_mix_g[i], mods[i][0], mods[i][1])

    def after_ffn(i):
        if i + 1 == depth:
            return ["plain"], [(final_norm_g,)]
        kinds, norms = [], []
        if i + 1 == n_a:
            kinds.append("mod")
            norms.append((kv_norm_g, sh_kv, sc_kv))
        kinds.append("mod")
        norms.append(mix_norm(i + 1))
        return kinds, norms

    h = _norm_mod(x2, *mix_norm(0), seq)
    kv = f_cum = None
    hk = flog_t = None
    out = None
    for i in range(depth):
        sh_m, sc_m, g_m, sh_f, sc_f, g_f = mods[i]
        ffn_norm = (norm_ffn_g[i], sh_f, sc_f)
        if i == n_a:
            kv = _mm(hk, kv_w.astype(BF16), 0, 2 * d, name="kv_proj")
            f_cum = _fcum(flog_t, forget_b, bsz, seq)
        if i < n_a:
            u = _mm(h, ssm_w_in, i, d, name="ssm_in_proj")
            bpack, cpack, coef = _s5_pack(ssm_log_step[i], ssm_a_re[i], ssm_a_im[i], ssm_b_re[i],
                                          ssm_b_im[i], ssm_c_re[i], ssm_c_im[i], _ssm_chunk(seq)[1])
            g = _ssm(u, bpack, cpack, coef, ssm_d[i].astype(F32), bsz, seq)
            z = _mm(g, ssm_glu_w, i, d, glu_bias=ssm_glu_b[i].astype(F32), name="ssm_glu")
            x2, h = _mm_res(z, ssm_w_out, i, x2, g_m, ["mod"], [ffn_norm], seq, name="ssm_out_proj")
        else:
            j = i - n_a
            q = _mm(h, attn_w_q, j, d, out_scale=HEAD_DIM ** -0.5 * LOG2E, name="q_proj")
            o = _attention(q, kv, f_cum, bsz, seq)
            x2, h = _mm_res(o, attn_w_out, j, x2, g_m, ["mod"], [ffn_norm], seq, name="attn_out_proj")
        kinds, norms = after_ffn(i)
        last = i + 1 == depth
        want_kv = (i + 1 == n_a) and not last
        wf_t = kv_w[:, 2 * d:].astype(F32).T if want_kv else None
        act, w_down_bf = _ffn_up(h, ffn_w_up, ffn_conv_w, ffn_conv_b, ffn_w_down, i, seq)
        res = _ffn_down(act, x2, w_down_bf, g_f, kinds, norms, seq, emit_x=not last, wf_t=wf_t)
        if last:
            out = res[0]
        else:
            x2 = res[0]
            if want_kv:
                hk, h, flog_t = res[1], res[2], res[3]
            else:
                h = res[1]
    return out.reshape(bsz, seq, d).astype(x.dtype)
```

```python
import functools
import math

import jax
import jax.numpy as jnp
import numpy as np
from jax import lax
from jax.experimental import pallas as pl
from jax.experimental.pallas import tpu as pltpu

F32 = jnp.float32
BF16 = jnp.bfloat16

LANES = 128
SUBLANES = 8
BF16_ROWS = 16
VMEM_LIMIT_BYTES = 56 * 1024 * 1024

SSM_GROUP = 16
SSM_STATE = 64
HEAD_DIM = 128
CONV_WIDTH = 3
NORM_EPS = 1e-6
LOG2E = math.log2(math.e)
N_MOD = 6

GROUPS_PER_BLOCK = LANES // SSM_GROUP
STATE_LANES = GROUPS_PER_BLOCK * SSM_STATE
SCAN_STEPS = (1, 2, 4)
SSM_CHUNK = 512
SSM_SLABS = 8
ATTN_HEADS = 4
FFN_ROWS = 1024
FFN_ROW_BLOCKS = 1
ROW_TILE = 512
MM_TILE = 1024
HIDDEN_BLOCK = 512
FFN_DOWN_K = 1536


def _params(*sem):
    return pltpu.CompilerParams(dimension_semantics=sem, vmem_limit_bytes=VMEM_LIMIT_BYTES)


def _tile(n, pref, align):
    if n <= pref:
        return n
    t = (pref // align) * align
    while t >= align:
        if n % t == 0:
            return t
        t -= align
    raise ValueError(f"no tile for {n}")


def _rms_hat(x):
    ms = jnp.mean(x * x, axis=-1, keepdims=True)
    return x * lax.rsqrt(ms + NORM_EPS)


def _modulate(y, shift, scale):
    return y * (1 + scale) + shift


def _gemv_kernel(c_ref, *rest, starts):
    njob = len(starts) - 1
    ins, outs, cs_ref = rest[:2 * njob], rest[2 * njob:3 * njob], rest[3 * njob]
    j = pl.program_id(0)

    @pl.when(j == 0)
    def _():
        cv = c_ref[...]
        cs_ref[...] = cv * jax.nn.sigmoid(cv)

    for q in range(njob):
        @pl.when(jnp.logical_and(j >= starts[q], j < starts[q + 1]))
        def _(q=q):
            w_ref, b_ref, o_ref = ins[2 * q], ins[2 * q + 1], outs[q]
            tn = w_ref.shape[1]
            for b in range(c_ref.shape[0]):
                cb = cs_ref[b]
                for lt in range(tn // LANES):
                    sl = slice(lt * LANES, (lt + 1) * LANES)
                    o_ref[b:b + 1, sl] = (jnp.sum(w_ref[:, sl] * cb, axis=0, keepdims=True)
                                          + b_ref[:, sl])


def _gemv_all(c_rep, jobs):
    bsz, k, _ = c_rep.shape
    tn = _tile(math.gcd(*[w.shape[2] for w, _, _ in jobs]), MM_TILE, LANES)
    counts = [w.shape[2] // tn for w, _, _ in jobs]
    starts = [sum(counts[:q]) for q in range(len(jobs) + 1)]
    in_specs = [pl.BlockSpec((bsz, k, LANES), lambda j: (0, 0, 0))]
    args, out_specs, out_shapes = [c_rep], [], []
    for q, (w, layer, b) in enumerate(jobs):
        col = lambda j, q=q: jnp.clip(j - starts[q], 0, counts[q] - 1)
        in_specs += [pl.BlockSpec((None, k, tn), lambda j, col=col, layer=layer: (layer, 0, col(j))),
                     pl.BlockSpec((1, tn), lambda j, col=col: (0, col(j)))]
        args += [w, b.reshape(1, -1)]
        out_specs.append(pl.BlockSpec((bsz, tn), lambda j, col=col: (0, col(j))))
        out_shapes.append(jax.ShapeDtypeStruct((bsz, w.shape[2]), F32))
    return pl.pallas_call(
        functools.partial(_gemv_kernel, starts=tuple(starts)),
        out_shape=out_shapes,
        grid=(starts[-1],),
        in_specs=in_specs,
        out_specs=out_specs,
        scratch_shapes=[pltpu.VMEM((bsz, k, LANES), F32)],
        compiler_params=_params("arbitrary"),
        name="adaln_gemv",
    )(*args)


def _norm_mod_kernel(x_ref, g_ref, sh_ref, sc_ref, o_ref):
    y = _rms_hat(x_ref[...]) * g_ref[...]
    o_ref[...] = _modulate(y, sh_ref[0], sc_ref[0]).astype(o_ref.dtype)


def _norm_mod(x2, g, shift, scale, seq):
    n, d = x2.shape
    tm = _tile(seq, MM_TILE, BF16_ROWS)
    tpb = seq // tm
    vec = pl.BlockSpec((1, 1, d), lambda i: (i // tpb, 0, 0))
    return pl.pallas_call(
        _norm_mod_kernel,
        out_shape=jax.ShapeDtypeStruct((n, d), BF16),
        grid=(n // tm,),
        in_specs=[pl.BlockSpec((tm, d), lambda i: (i, 0)),
                  pl.BlockSpec((1, d), lambda i: (0, 0)), vec, vec],
        out_specs=pl.BlockSpec((tm, d), lambda i: (i, 0)),
        compiler_params=_params("arbitrary"),
        name="norm_mod",
    )(x2, g.reshape(1, d), shift, scale)


def _mm_kernel(a_ref, w_ref, *rest, glu, out_scale):
    *rest, wb_ref = rest

    @pl.when(pl.program_id(1) == 0)
    def _():
        wb_ref[...] = w_ref[...].astype(BF16)

    acc = jnp.dot(a_ref[...], wb_ref[...], preferred_element_type=F32)
    if glu:
        g_ref, b_ref, o_ref = rest
        o_ref[...] = (g_ref[...].astype(F32) * jax.nn.sigmoid(acc + b_ref[...])).astype(o_ref.dtype)
    else:
        (o_ref,) = rest
        if out_scale != 1.0:
            acc = acc * out_scale
        o_ref[...] = acc.astype(o_ref.dtype)


def _mm(a, w_stack, layer, nout, *, glu_bias=None, out_scale=1.0, name):
    n, k = a.shape
    tm = _tile(n, MM_TILE, BF16_ROWS)
    tn = _tile(nout, MM_TILE, LANES)
    if w_stack.ndim == 2:
        w_spec = pl.BlockSpec((k, tn), lambda j, i: (0, j))
    else:
        w_spec = pl.BlockSpec((None, k, tn), lambda j, i: (layer, 0, j))
    in_specs = [pl.BlockSpec((tm, k), lambda j, i: (i, 0)), w_spec]
    args = [a, w_stack]
    if glu_bias is not None:
        in_specs += [pl.BlockSpec((tm, tn), lambda j, i: (i, j)),
                     pl.BlockSpec((1, tn), lambda j, i: (0, j))]
        args += [a, glu_bias.reshape(1, nout)]
    return pl.pallas_call(
        functools.partial(_mm_kernel, glu=glu_bias is not None, out_scale=out_scale),
        out_shape=jax.ShapeDtypeStruct((n, nout), BF16),
        grid=(nout // tn, n // tm),
        in_specs=in_specs,
        out_specs=pl.BlockSpec((tm, tn), lambda j, i: (i, j)),
        scratch_shapes=[pltpu.VMEM((k, tn), BF16)],
        compiler_params=_params("arbitrary", "arbitrary"),
        name=name,
    )(*args)


def _norm_arg_count(kinds):
    return sum(3 if kd == "mod" else 1 for kd in kinds)


def _emit_norms(xn, kinds, norm_refs, out_refs):
    xhat = _rms_hat(xn)
    first = None
    p = 0
    for kd, o_ref in zip(kinds, out_refs):
        if kd == "mod":
            g_ref, sh_ref, sc_ref = norm_refs[p:p + 3]
            p += 3
            val = _modulate(xhat * g_ref[...], sh_ref[0], sc_ref[0])
        else:
            val = xhat * norm_refs[p][...]
            p += 1
        if first is None:
            first = val
        o_ref[...] = val.astype(o_ref.dtype)
    return first


def _norm_specs_args(kinds, norms, d, bidx):
    specs, args = [], []
    for kd, nm in zip(kinds, norms):
        specs.append(pl.BlockSpec((1, d), lambda *ix: (0, 0)))
        args.append(nm[0].reshape(1, d))
        if kd == "mod":
            for v in nm[1:]:
                specs.append(pl.BlockSpec((1, 1, d), bidx))
                args.append(v)
    return specs, args


def _norm_outs(kinds, n, d, tm, row_map):
    shapes = [jax.ShapeDtypeStruct((n, d), BF16 if kd == "mod" else F32) for kd in kinds]
    specs = [pl.BlockSpec((tm, d), row_map) for _ in kinds]
    return shapes, specs


def _mm_res_kernel(a_ref, w_ref, x_ref, gate_ref, *rest, kinds):
    *rest, wb_ref = rest
    na = _norm_arg_count(kinds)
    norm_refs, outs = rest[:na], rest[na:]

    @pl.when(pl.program_id(0) == 0)
    def _():
        wb_ref[...] = w_ref[...].astype(BF16)

    acc = jnp.dot(a_ref[...], wb_ref[...], preferred_element_type=F32)
    xn = x_ref[...] + gate_ref[0] * acc
    outs[0][...] = xn
    _emit_norms(xn, kinds, norm_refs, outs[1:])


def _mm_res(a, w_stack, layer, x2, gate, kinds, norms, seq, *, name):
    n, k = a.shape
    d = w_stack.shape[2]
    tm = _tile(seq, ROW_TILE, BF16_ROWS)
    tpb = seq // tm
    bidx = lambda i: (i // tpb, 0, 0)
    row = lambda i: (i, 0)
    nspecs, nargs = _norm_specs_args(kinds, norms, d, bidx)
    oshapes, ospecs = _norm_outs(kinds, n, d, tm, row)
    return pl.pallas_call(
        functools.partial(_mm_res_kernel, kinds=tuple(kinds)),
        out_shape=[jax.ShapeDtypeStruct((n, d), F32)] + oshapes,
        grid=(n // tm,),
        in_specs=[pl.BlockSpec((tm, k), row),
                  pl.BlockSpec((None, k, d), lambda i: (layer, 0, 0), pipeline_mode=pl.Buffered(1)),
                  pl.BlockSpec((tm, d), row),
                  pl.BlockSpec((1, 1, d), bidx)] + nspecs,
        out_specs=[pl.BlockSpec((tm, d), row)] + ospecs,
        scratch_shapes=[pltpu.VMEM((k, d), BF16)],
        compiler_params=_params("arbitrary"),
        name=name,
    )(a, w_stack, x2, gate, *nargs)


def _ssm_kernel(u_ref, bp_ref, cp_ref, coef_ref, d_ref, o_ref, upad_ref, st_ref, ypad_ref,
                carry_ref, *, slabs, seg_len):
    s = STATE_LANES
    nseg = SUBLANES
    pitch = seg_len + 1
    nlt = s // LANES

    @pl.when(pl.program_id(2) == 0)
    def _():
        carry_ref[...] = jnp.zeros_like(carry_ref)

    def cmul(ar, ai, xr, xi):
        return ar * xr - ai * xi, ar * xi + ai * xr

    def load_rows(sb, half, rows):
        return jnp.concatenate([st_ref[sb, half * nlt + j, rows, :] for j in range(nlt)], axis=-1)

    def store_rows(sb, half, rows, val):
        for j in range(nlt):
            st_ref[sb, half * nlt + j, rows, :] = val[:, j * LANES:(j + 1) * LANES]

    for sb in range(slabs):
        ch = slice(sb * LANES, (sb + 1) * LANES)
        u32 = u_ref[:, ch].astype(F32)
        upad_ref[sb] = jnp.zeros(upad_ref.shape[1:], F32)
        for k in range(nseg):
            upad_ref[sb, pl.ds(k * pitch, seg_len), :] = u32[k * seg_len:(k + 1) * seg_len]
        bu = jnp.dot(upad_ref[sb].astype(BF16), bp_ref[sb], preferred_element_type=F32)
        for j in range(2 * nlt):
            st_ref[sb, j] = bu[:, j * LANES:(j + 1) * LANES]
        ar, ai = coef_ref[sb, 0], coef_ref[sb, 1]

        sr = jnp.zeros((nseg, s), F32)
        si = jnp.zeros((nseg, s), F32)
        for t in range(seg_len):
            rows = pl.ds(t, nseg, stride=pitch)
            pr_, pi_ = cmul(ar, ai, sr, si)
            sr = pr_ + load_rows(sb, 0, rows)
            si = pi_ + load_rows(sb, 1, rows)
            store_rows(sb, 0, rows, sr)
            store_rows(sb, 1, rows, si)

        tr, ti = sr, si
        for q, dlt in enumerate(SCAN_STEPS):
            hr, hi = coef_ref[sb, 2 + 2 * q], coef_ref[sb, 3 + 2 * q]
            dr, di = cmul(hr, hi, pltpu.roll(tr, dlt, 0), pltpu.roll(ti, dlt, 0))
            tr, ti = tr + dr, ti + di
        cr, ci = carry_ref[sb, 0], carry_ref[sb, 1]
        dr, di = cmul(coef_ref[sb, 8], coef_ref[sb, 9], cr, ci)
        tr, ti = tr + dr, ti + di
        row0 = lax.broadcasted_iota(jnp.int32, (nseg, s), 0) == 0
        zr = jnp.where(row0, cr, pltpu.roll(tr, 1, 0))
        zi = jnp.where(row0, ci, pltpu.roll(ti, 1, 0))
        carry_ref[sb, 0] = jnp.broadcast_to(tr[nseg - 1:nseg], tr.shape)
        carry_ref[sb, 1] = jnp.broadcast_to(ti[nseg - 1:nseg], ti.shape)

        for t in range(seg_len):
            rows = pl.ds(t, nseg, stride=pitch)
            zr, zi = cmul(ar, ai, zr, zi)
            store_rows(sb, 0, rows, load_rows(sb, 0, rows) + zr)
            store_rows(sb, 1, rows, load_rows(sb, 1, rows) + zi)

        states = jnp.concatenate([st_ref[sb, j] for j in range(2 * nlt)], axis=-1).astype(BF16)
        ypad_ref[sb] = jnp.dot(states, cp_ref[sb], preferred_element_type=F32)
        y = jnp.concatenate([ypad_ref[sb, pl.ds(k * pitch, seg_len), :] for k in range(nseg)], axis=0)
        y = y + d_ref[:, ch] * u32
        o_ref[:, ch] = jax.nn.gelu(y).astype(o_ref.dtype)


def _ssm_chunk(seq):
    tt = _tile(seq, SSM_CHUNK, SUBLANES * SUBLANES)
    return tt, tt // SUBLANES


def _s5_pack(log_step, a_re, a_im, b_re, b_im, c_re, c_im, seg_len):
    g, p = a_re.shape
    h = b_re.shape[-1]
    nblk = g // GROUPS_PER_BLOCK
    step = jnp.exp(log_step.astype(F32))[:, None]
    lam_re, lam_im = a_re.astype(F32), a_im.astype(F32)

    rows = np.arange(SUBLANES)
    expo = np.stack([np.ones(SUBLANES)] + [np.full(SUBLANES, seg_len * dlt) for dlt in SCAN_STEPS]
                    + [seg_len * (rows + 1.0)]).astype(np.float32)
    keep = np.stack([np.ones(SUBLANES)] + [rows >= dlt for dlt in SCAN_STEPS]
                    + [np.ones(SUBLANES)]).astype(np.float32)
    kk = expo[:, :, None, None]
    mag = jnp.exp(kk * (lam_re * step)) * keep[:, :, None, None]
    ang = kk * (lam_im * step)
    pow_re, pow_im = mag * jnp.cos(ang), mag * jnp.sin(ang)
    coef = jnp.stack([pow_re, pow_im], axis=1).reshape(len(expo), 2, SUBLANES, nblk, STATE_LANES)
    coef = coef.transpose(3, 0, 1, 2, 4).reshape(nblk, 2 * len(expo), SUBLANES, STATE_LANES)

    abar_re, abar_im = pow_re[0, 0], pow_im[0, 0]
    den = lam_re * lam_re + lam_im * lam_im
    nr = abar_re - 1.0
    fr = (nr * lam_re + abar_im * lam_im) / den
    fi = (abar_im * lam_re - nr * lam_im) / den
    br, bi = b_re.astype(F32), b_im.astype(F32)
    bbar_re = fr[..., None] * br - fi[..., None] * bi
    bbar_im = fr[..., None] * bi + fi[..., None] * br
    eye = jnp.eye(GROUPS_PER_BLOCK, dtype=F32)
    bb = jnp.stack([bbar_re, bbar_im]).reshape(2, nblk, GROUPS_PER_BLOCK, p, h)
    bpack = jnp.einsum("zjgph,gk->jghzkp", bb, eye).reshape(nblk, LANES, 2 * STATE_LANES)
    cc = jnp.stack([c_re.astype(F32), -c_im.astype(F32)]).reshape(2, nblk, GROUPS_PER_BLOCK, h, p)
    cpack = jnp.einsum("zjghp,gk->jzgpkh", cc, eye).reshape(nblk, 2 * STATE_LANES, LANES)
    return bpack.astype(BF16), cpack.astype(BF16), coef


def _ssm(u, bpack, cpack, coef, dvec, bsz, seq):
    n, d = u.shape
    nblk = d // LANES
    tt, seg_len = _ssm_chunk(seq)
    nt = seq // tt
    slabs = next(c for c in (SSM_SLABS, 2, 1) if nblk % c == 0)
    width = slabs * LANES
    prow = -(-(SUBLANES * (seg_len + 1)) // BF16_ROWS) * BF16_ROWS
    row = lambda b, j, t: (b * nt + t, j)
    return pl.pallas_call(
        functools.partial(_ssm_kernel, slabs=slabs, seg_len=seg_len),
        out_shape=jax.ShapeDtypeStruct((n, d), BF16),
        grid=(bsz, nblk // slabs, nt),
        in_specs=[pl.BlockSpec((tt, width), row),
                  pl.BlockSpec((slabs, LANES, 2 * STATE_LANES), lambda b, j, t: (j, 0, 0)),
                  pl.BlockSpec((slabs, 2 * STATE_LANES, LANES), lambda b, j, t: (j, 0, 0)),
                  pl.BlockSpec((slabs, 2 * len(SCAN_STEPS) + 4, SUBLANES, STATE_LANES),
                               lambda b, j, t: (j, 0, 0, 0)),
                  pl.BlockSpec((1, width), lambda b, j, t: (0, j))],
        out_specs=pl.BlockSpec((tt, width), row),
        scratch_shapes=[pltpu.VMEM((slabs, prow, LANES), F32),
                        pltpu.VMEM((slabs, 2 * STATE_LANES // LANES, prow, LANES), F32),
                        pltpu.VMEM((slabs, prow, LANES), F32),
                        pltpu.VMEM((slabs, 2, SUBLANES, STATE_LANES), F32)],
        compiler_params=_params("arbitrary", "arbitrary", "arbitrary"),
        name="s5_ssm",
    )(u, bpack, cpack, coef, dvec.reshape(1, d))


def _ffn_up_kernel(h_ref, wu_ref, wv_ref, cwu_ref, cwv_ref, cbu_ref, cbv_ref, wd_ref, o_ref,
                   wdb_ref, wub_ref, wvb_ref, carry_ref, *, tiles_per_batch, nsub):
    i = pl.program_id(1)
    tm = h_ref.shape[0]
    wdb_ref[...] = wd_ref[...].astype(BF16)

    @pl.when(i == 0)
    def _():
        wub_ref[...] = wu_ref[...].astype(BF16)
        wvb_ref[...] = wv_ref[...].astype(BF16)

    first = (i % tiles_per_batch) == 0
    prevs = []
    for stream in range(2):
        pv = carry_ref[stream]
        prevs.append(jnp.where(first, jnp.zeros_like(pv), pv))
    ts = tm // nsub
    for sb in range(nsub):
        rows = slice(sb * ts, (sb + 1) * ts)
        h = h_ref[rows, :]
        conv = []
        for stream, (wb_ref, cw_ref, cb_ref) in enumerate(((wub_ref, cwu_ref, cbu_ref),
                                                            (wvb_ref, cwv_ref, cbv_ref))):
            a = jnp.dot(h, wb_ref[...], preferred_element_type=F32)
            ext = jnp.concatenate([prevs[stream], a], axis=0)
            prevs[stream] = a[ts - SUBLANES:]
            cw = cw_ref[...]
            out = cw[CONV_WIDTH - 1:CONV_WIDTH] * a
            for lag in range(1, CONV_WIDTH):
                tap = CONV_WIDTH - 1 - lag
                out = out + cw[tap:tap + 1] * pltpu.roll(ext, lag, 0)[SUBLANES:]
            conv.append(out + cb_ref[...])
        cu, cv = conv
        o_ref[rows, :] = (cu * jax.nn.sigmoid(cu) * cv).astype(o_ref.dtype)
    for stream in range(2):
        carry_ref[stream] = prevs[stream]


def _ffn_up(h, w_up, conv_w, conv_b, w_down, layer, seq):
    n, d = h.shape
    ff = w_up.shape[2] // 2
    ts = _tile(seq, FFN_ROWS, BF16_ROWS)
    nsub = FFN_ROW_BLOCKS if seq % (FFN_ROW_BLOCKS * ts) == 0 else 1
    tm = nsub * ts
    tf = _tile(ff, HIDDEN_BLOCK, LANES)
    nf = ff // tf
    ntile = n // tm
    wrows = ff // (nf * ntile)
    assert wrows * nf * ntile == ff and wrows % BF16_ROWS == 0
    cb = conv_b.astype(F32).reshape(conv_b.shape[0], 1, 2 * ff)
    return pl.pallas_call(
        functools.partial(_ffn_up_kernel, tiles_per_batch=seq // tm, nsub=nsub),
        out_shape=[jax.ShapeDtypeStruct((n, ff), BF16),
                   jax.ShapeDtypeStruct((ff, w_down.shape[2]), BF16)],
        grid=(nf, ntile),
        in_specs=[pl.BlockSpec((tm, d), lambda f, i: (i, 0)),
                  pl.BlockSpec((None, d, tf), lambda f, i: (layer, 0, f)),
                  pl.BlockSpec((None, d, tf), lambda f, i: (layer, 0, nf + f)),
                  pl.BlockSpec((None, CONV_WIDTH, tf), lambda f, i: (layer, 0, f)),
                  pl.BlockSpec((None, CONV_WIDTH, tf), lambda f, i: (layer, 0, nf + f)),
                  pl.BlockSpec((None, 1, tf), lambda f, i: (layer, 0, f)),
                  pl.BlockSpec((None, 1, tf), lambda f, i: (layer, 0, nf + f)),
                  pl.BlockSpec((None, wrows, w_down.shape[2]), lambda f, i: (layer, f * ntile + i, 0))],
        out_specs=[pl.BlockSpec((tm, tf), lambda f, i: (i, f)),
                   pl.BlockSpec((wrows, w_down.shape[2]), lambda f, i: (f * ntile + i, 0))],
        scratch_shapes=[pltpu.VMEM((d, tf), BF16), pltpu.VMEM((d, tf), BF16),
                        pltpu.VMEM((2, SUBLANES, tf), F32)],
        compiler_params=_params("arbitrary", "arbitrary"),
        name="ffn_up_conv",
    )(h, w_up, w_up, conv_w.astype(F32), conv_w.astype(F32), cb, cb, w_down)


def _ffn_down_kernel(a_ref, w_ref, x_ref, gate_ref, *rest, kinds, emit_x, with_flog):
    na = _norm_arg_count(kinds)
    norm_refs = rest[:na]
    rest = rest[na:]
    if with_flog:
        wf_ref, rest = rest[0], rest[1:]
    outs = rest
    acc_ref = outs[0]
    k = pl.program_id(1)

    def contrib():
        return jnp.dot(a_ref[...], w_ref[...], preferred_element_type=F32)

    @pl.when(k == 0)
    def _():
        acc_ref[...] = contrib()

    @pl.when(k > 0)
    def _():
        acc_ref[...] += contrib()

    @pl.when(k == pl.num_programs(1) - 1)
    def _():
        xn = x_ref[...] + gate_ref[0] * acc_ref[...]
        q = 0
        if emit_x:
            outs[0][...] = xn
            q = 1
        first = _emit_norms(xn, kinds, norm_refs, outs[q:q + len(kinds)])
        if with_flog:
            hi = first.astype(BF16)
            lo = (first - hi.astype(F32)).astype(BF16)
            wf = wf_ref[...]
            whi = wf.astype(BF16)
            wlo = (wf - whi.astype(F32)).astype(BF16)
            nt = (((1,), (1,)), ((), ()))
            nh = wf.shape[0]
            both = lax.dot_general(jnp.concatenate([whi, wlo], axis=0), hi, nt,
                                   preferred_element_type=F32)
            outs[-1][...] = (both[:nh] + both[nh:]
                             + lax.dot_general(whi, lo, nt, preferred_element_type=F32))


def _ffn_down(act, x2, w_down, gate, kinds, norms, seq, *, emit_x, wf_t=None):
    n, ff = act.shape
    d = w_down.shape[1]
    tm = _tile(seq, ROW_TILE, BF16_ROWS)
    tk = _tile(ff, FFN_DOWN_K, LANES)
    tpb = seq // tm
    row = lambda i, k: (i, 0)
    bidx = lambda i, k: (i // tpb, 0, 0)
    nspecs, nargs = _norm_specs_args(kinds, norms, d, bidx)
    oshapes, ospecs = _norm_outs(kinds, n, d, tm, row)
    if emit_x:
        oshapes = [jax.ShapeDtypeStruct((n, d), F32)] + oshapes
        ospecs = [pl.BlockSpec((tm, d), row)] + ospecs
    assert oshapes[0].dtype == F32
    extra_specs, extra_args = [], []
    if wf_t is not None:
        nh = wf_t.shape[0]
        extra_specs = [pl.BlockSpec((nh, d), lambda i, k: (0, 0))]
        extra_args = [wf_t]
        oshapes = oshapes + [jax.ShapeDtypeStruct((nh, n), F32)]
        ospecs = ospecs + [pl.BlockSpec((nh, tm), lambda i, k: (0, i))]
    return pl.pallas_call(
        functools.partial(_ffn_down_kernel, kinds=tuple(kinds), emit_x=emit_x,
                          with_flog=wf_t is not None),
        out_shape=oshapes,
        grid=(n // tm, ff // tk),
        in_specs=[pl.BlockSpec((tm, tk), lambda i, k: (i, k)),
                  pl.BlockSpec((tk, d), lambda i, k: (k, 0)),
                  pl.BlockSpec((tm, d), row),
                  pl.BlockSpec((1, 1, d), bidx)] + nspecs + extra_specs,
        out_specs=ospecs,
        compiler_params=_params("arbitrary", "arbitrary"),
        name="ffn_down_res",
    )(act, w_down, x2, gate, *nargs, *extra_args)


def _fcum_kernel(z_ref, b_ref, tri_ref, o_ref):
    z = z_ref[...] + b_ref[...]
    lf = jnp.minimum(z, 0.0) - jnp.log1p(jnp.exp(-jnp.abs(z)))
    carry = jnp.zeros((z.shape[0], 1), F32)
    for cidx in range(z.shape[1] // LANES):
        sl = slice(cidx * LANES, (cidx + 1) * LANES)
        cs = jnp.dot(lf[:, sl], tri_ref[...], precision=lax.Precision.HIGHEST,
                     preferred_element_type=F32) + carry
        o_ref[:, sl] = cs * LOG2E
        carry = cs[:, LANES - 1:LANES]


def _fcum(flog_t, forget_b, bsz, seq):
    nh, n = flog_t.shape
    tri = (jnp.arange(LANES)[:, None] <= jnp.arange(LANES)[None, :]).astype(F32)
    return pl.pallas_call(
        _fcum_kernel,
        out_shape=jax.ShapeDtypeStruct((nh, n), F32),
        grid=(bsz,),
        in_specs=[pl.BlockSpec((nh, seq), lambda b: (0, b)),
                  pl.BlockSpec((nh, 1), lambda b: (0, 0)),
                  pl.BlockSpec((LANES, LANES), lambda b: (0, 0))],
        out_specs=pl.BlockSpec((nh, seq), lambda b: (0, b)),
        compiler_params=_params("arbitrary"),
        name="forget_cumsum",
    )(flog_t, forget_b.astype(F32).reshape(nh, 1), tri)


def _attn_kernel(q_ref, k_ref, v_ref, f_ref, mask_ref, o_ref, m_ref, l_ref, acc_ref, *, tk, heads):
    qi = pl.program_id(2)
    m_ref[...] = jnp.full_like(m_ref, -jnp.inf)
    l_ref[...] = jnp.zeros_like(l_ref)
    acc_ref[...] = jnp.zeros_like(acc_ref)
    ncol = tk // LANES

    def head_step(hd, cidx, masked):
        hs = slice(hd * HEAD_DIM, (hd + 1) * HEAD_DIM)
        k0 = pl.multiple_of(cidx * tk, tk)
        kc = k_ref[pl.ds(k0, tk), hs]
        vc = v_ref[pl.ds(k0, tk), hs]
        s = lax.dot_general(q_ref[:, hs], kc, (((1,), (1,)), ((), ())), preferred_element_type=F32)
        s = s - f_ref[hd, :, pl.ds(k0, tk)]
        if masked:
            s = s + mask_ref[...]
        cols_s = [s[:, j * LANES:(j + 1) * LANES] for j in range(ncol)]
        mc = cols_s[0]
        for sj in cols_s[1:]:
            mc = jnp.maximum(mc, sj)
        m_old = m_ref[hd]
        m_new = jnp.maximum(m_old, jnp.max(mc, axis=-1, keepdims=True))
        alpha = jnp.exp2(m_old - m_new)
        ps = [jnp.exp2(sj - m_new) for sj in cols_s]
        lsum = ps[0]
        for pj in ps[1:]:
            lsum = lsum + pj
        l_ref[hd] = alpha * l_ref[hd] + lsum
        p = jnp.concatenate(ps, axis=-1).astype(BF16)
        acc_ref[hd] = alpha * acc_ref[hd] + jnp.dot(p, vc, preferred_element_type=F32)
        m_ref[hd] = m_new

    def step(cidx, masked):
        for hd in range(heads):
            head_step(hd, cidx, masked)

    n_full = qi

    def pair_body(pidx, carry):
        step(2 * pidx, False)
        step(2 * pidx + 1, False)
        return carry

    lax.fori_loop(0, n_full // 2, pair_body, 0)

    @pl.when(n_full % 2 == 1)
    def _():
        step(n_full - 1, False)

    step(n_full, True)
    for hd in range(heads):
        l_row = jnp.sum(l_ref[hd], axis=-1, keepdims=True)
        o_ref[:, hd * HEAD_DIM:(hd + 1) * HEAD_DIM] = (acc_ref[hd] / l_row).astype(o_ref.dtype)


def _attention(q, kv, f_cum, bsz, seq):
    n, d = q.shape
    nh = d // HEAD_DIM
    heads = next(c for c in (ATTN_HEADS, 2, 1) if nh % c == 0)
    tq = _tile(seq, ROW_TILE, LANES)
    tk = tq
    nq = seq // tq
    width = heads * HEAD_DIM
    ngrp = nh // heads
    causal = jnp.where(jnp.arange(tk)[None, :] <= jnp.arange(tq)[:, None], 0.0, -jnp.inf).astype(F32)
    return pl.pallas_call(
        functools.partial(_attn_kernel, tk=tk, heads=heads),
        out_shape=jax.ShapeDtypeStruct((n, d), BF16),
        grid=(bsz, ngrp, nq),
        in_specs=[pl.BlockSpec((tq, width), lambda b, h, i: (b * nq + i, h)),
                  pl.BlockSpec((seq, width), lambda b, h, i: (b, h)),
                  pl.BlockSpec((seq, width), lambda b, h, i: (b, ngrp + h)),
                  pl.BlockSpec((heads, 1, seq), lambda b, h, i: (h, 0, b)),
                  pl.BlockSpec((tq, tk), lambda b, h, i: (0, 0))],
        out_specs=pl.BlockSpec((tq, width), lambda b, h, i: (b * nq + i, h)),
        scratch_shapes=[pltpu.VMEM((heads, tq, LANES), F32), pltpu.VMEM((heads, tq, LANES), F32),
                        pltpu.VMEM((heads, tq, HEAD_DIM), F32)],
        compiler_params=_params("arbitrary", "arbitrary", "arbitrary"),
        name="fox_attention",
    )(q, kv, kv, f_cum.reshape(nh, 1, n), causal)


def kernel(x, c, mod_w, mod_b, norm_mix_g, norm_ffn_g, ssm_w_in, ssm_log_step, ssm_a_re, ssm_a_im,
           ssm_b_re, ssm_b_im, ssm_c_re, ssm_c_im, ssm_d, ssm_glu_w, ssm_glu_b, ssm_w_out,
           kv_mod_w, kv_mod_b, kv_norm_g, kv_w, forget_b, attn_w_q, attn_w_out,
           ffn_w_up, ffn_conv_w, ffn_conv_b, ffn_w_down, final_norm_g):
    bsz, seq, d = x.shape
    n = bsz * seq
    depth = mod_w.shape[0]
    n_a = ssm_w_in.shape[0]
    assert n_a >= 1 and d % LANES == 0
    x2 = x.astype(F32).reshape(n, d)

    c_rep = jnp.broadcast_to(c.astype(F32)[:, :, None], (bsz, d, LANES))

    def vecs(m, count):
        return [m[:, k * d:(k + 1) * d].reshape(bsz, 1, d) for k in range(count)]

    jobs = [(mod_w, i, mod_b[i]) for i in range(depth)]
    jobs.append((kv_mod_w.reshape(1, *kv_mod_w.shape), 0, kv_mod_b))
    gemv_out = _gemv_all(c_rep, jobs)
    mods = [vecs(m, N_MOD) for m in gemv_out[:depth]]
    sh_kv, sc_kv = vecs(gemv_out[depth], 2)

    def mix_norm(i):
        return (norm_mix_g[i], mods[i][0], mods[i][1])

    def after_ffn(i):
        if i + 1 == depth:
            return ["plain"], [(final_norm_g,)]
        kinds, norms = [], []
        if i + 1 == n_a:
            kinds.append("mod")
            norms.append((kv_norm_g, sh_kv, sc_kv))
        kinds.append("mod")
        norms.append(mix_norm(i + 1))
        return kinds, norms

    h = _norm_mod(x2, *mix_norm(0), seq)
    kv = f_cum = None
    hk = flog_t = None
    out = None
    for i in range(depth):
        sh_m, sc_m, g_m, sh_f, sc_f, g_f = mods[i]
        ffn_norm = (norm_ffn_g[i], sh_f, sc_f)
        if i == n_a:
            kv = _mm(hk, kv_w, 0, 2 * d, name="kv_proj")
            f_cum = _fcum(flog_t, forget_b, bsz, seq)
        if i < n_a:
            u = _mm(h, ssm_w_in, i, d, name="ssm_in_proj")
            bpack, cpack, coef = _s5_pack(ssm_log_step[i], ssm_a_re[i], ssm_a_im[i], ssm_b_re[i],
                                          ssm_b_im[i], ssm_c_re[i], ssm_c_im[i], _ssm_chunk(seq)[1])
            g = _ssm(u, bpack, cpack, coef, ssm_d[i].astype(F32), bsz, seq)
            z = _mm(g, ssm_glu_w, i, d, glu_bias=ssm_glu_b[i].astype(F32), name="ssm_glu")
            x2, h = _mm_res(z, ssm_w_out, i, x2, g_m, ["mod"], [ffn_norm], seq, name="ssm_out_proj")
        else:
            j = i - n_a
            q = _mm(h, attn_w_q, j, d, out_scale=HEAD_DIM ** -0.5 * LOG2E, name="q_proj")
            o = _attention(q, kv, f_cum, bsz, seq)
            x2, h = _mm_res(o, attn_w_out, j, x2, g_m, ["mod"], [ffn_norm], seq, name="attn_out_proj")
        kinds, norms = after_ffn(i)
        last = i + 1 == depth
        want_kv = (i + 1 == n_a) and not last
        wf_t = kv_w[:, 2 * d:].astype(F32).T if want_kv else None
        act, w_down_bf = _ffn_up(h, ffn_w_up, ffn_conv_w, ffn_conv_b, ffn_w_down, i, seq)
        res = _ffn_down(act, x2, w_down_bf, g_f, kinds, norms, seq, emit_x=not last, wf_t=wf_t)
        if last:
            out = res[0]
        else:
            x2 = res[0]
            if want_kv:
                hk, h, flog_t = res[1], res[2], res[3]
            else:
                h = res[1]
    return out.reshape(bsz, seq, d).astype(x.dtype)
```
